```python
import math
import jax, jax.numpy as jnp
from jax import lax
import numpy as np

D_MODEL = 2048
BATCH = 8
SEQ = 4096
DEPTH = 1
DEC_BATCH = 2
DEC_SEQ = 16384
PAST_LEN = 128

ATT_HEADS = 8
ATT_V_DIM = D_MODEL // 16
ATT_QK_DIM = ATT_V_DIM // 2
Q_BLOCK = 128
N_BUCKETS = 32
MAX_DISTANCE = 128
D_INNER = D_MODEL // 2
SSM_HEADDIM = 64
SSM_HEADS = D_INNER // SSM_HEADDIM
SSM_GROUPS = 2
HEADS_PER_GROUP = SSM_HEADS // SSM_GROUPS
D_STATE = 128
D_CONV = 5
CONV_PAD = D_CONV // 2
CONV_DIM = D_INNER + 2 * SSM_GROUPS * D_STATE
SSM_CHUNK = 128
N_MEM = 256
CROSS_HEADS = 4
CROSS_DIM = D_MODEL // CROSS_HEADS
PEER_HEADS = 8
N_KEYS = 128
N_EXPERTS = N_KEYS * N_KEYS
PEER_TOPK = 16
D_KEY = 128
D_KEY_HALF = D_KEY // 2
TOK_BLOCK = 128
EPS = 1e-6

Q_COLS = ATT_HEADS * 2 * ATT_QK_DIM
K_COLS = ATT_HEADS * 2 * ATT_QK_DIM
V_COLS = ATT_HEADS * ATT_V_DIM
Z_COLS = D_INNER
XBC_COLS = CONV_DIM
DT_COLS = 2 * SSM_HEADS
GATE_COLS = 2 * D_MODEL
IN_COLS = Q_COLS + K_COLS + V_COLS + Z_COLS + XBC_COLS + DT_COLS + GATE_COLS
SPLITS = list(np.cumsum([Q_COLS, K_COLS, V_COLS, Z_COLS, XBC_COLS, DT_COLS]))

kernel_name = 'hybrid_diffattn_ssd_peer_encoder'


def _rmsnorm(x, g):
    xf = x.astype(jnp.float32)
    y = xf * lax.rsqrt(jnp.mean(xf * xf, axis=-1, keepdims=True) + EPS)
    return (y * g.astype(jnp.float32)).astype(x.dtype)


def _t5_bucket(rel):
    half = N_BUCKETS // 2
    exact = half // 2
    n = jnp.abs(rel)
    far = exact + (jnp.log(jnp.maximum(n, 1).astype(jnp.float32) / exact)
                   / math.log(MAX_DISTANCE / exact) * (half - exact)).astype(jnp.int32)
    far = jnp.minimum(far, half - 1)
    return jnp.where(rel > 0, half, 0) + jnp.where(n < exact, n, far)


def _diff_attention(q, k, v, rel_bias, lam):
    b, s = q.shape[0], q.shape[1]
    nb = s // Q_BLOCK
    qb = q.reshape(b, nb, Q_BLOCK, ATT_HEADS, 2, ATT_QK_DIM).transpose(1, 0, 2, 3, 4, 5)
    k_pos = jnp.arange(s, dtype=jnp.int32)
    scale = ATT_QK_DIM ** -0.5

    def block(args):
        q_blk, i = args
        q_pos = i * Q_BLOCK + jnp.arange(Q_BLOCK, dtype=jnp.int32)
        bias = rel_bias[_t5_bucket(k_pos[None, :] - q_pos[:, None])]
        bias = bias.astype(jnp.float32).transpose(2, 0, 1)
        sc = jnp.einsum('bqhcd,bkhcd->bhcqk', q_blk, k).astype(jnp.float32) * scale
        pr = jax.nn.softmax(sc + bias[None, :, None], axis=-1)
        w = pr[:, :, 0] - lam * pr[:, :, 1]
        return jnp.einsum('bhqk,bkhd->bqhd', w.astype(v.dtype), v)

    out = lax.map(block, (qb, jnp.arange(nb, dtype=jnp.int32)))
    return out.transpose(1, 0, 2, 3, 4).reshape(b, s, ATT_HEADS, ATT_V_DIM)


def _dwconv(x, w, bias):
    y = lax.conv_general_dilated(x, w[:, None, :], (1,), [(CONV_PAD, CONV_PAD)],
                                 dimension_numbers=('NWC', 'WIO', 'NWC'),
                                 feature_group_count=x.shape[-1])
    return y + bias


def _ssd(x, dt, A, B, C):
    b, l, g, r, p = x.shape
    n = B.shape[-1]
    c = l // SSM_CHUNK
    xc = (x * dt[..., None]).reshape(b, c, SSM_CHUNK, g, r, p)
    a = (dt * A).reshape(b, c, SSM_CHUNK, g, r).transpose(0, 3, 4, 1, 2)
    Bc = B.reshape(b, c, SSM_CHUNK, g, n)
    Cc = C.reshape(b, c, SSM_CHUNK, g, n)
    a_cs = jnp.cumsum(a, axis=-1)
    lower = jnp.tril(jnp.ones((SSM_CHUNK, SSM_CHUNK), dtype=bool))
    seg = jnp.exp(jnp.where(lower, a_cs[..., :, None] - a_cs[..., None, :], -jnp.inf))
    cb = jnp.einsum('bclgn,bcsgn->bgcls', Cc, Bc)
    y_diag = jnp.einsum('bgrcls,bcsgrp->bclgrp', seg * cb[:, :, None], xc)
    decay = jnp.exp(a_cs[..., -1:] - a_cs)
    states = jnp.einsum('bclgn,bgrcl,bclgrp->cbgrpn', Bc, decay, xc)
    chunk_decay = jnp.exp(a_cs[..., -1]).transpose(3, 0, 1, 2)

    def step(hst, inp):
        s_c, d_c = inp
        return hst * d_c[..., None, None] + s_c, hst

    _, prev = lax.scan(step, jnp.zeros(states.shape[1:], states.dtype), (states, chunk_decay))
    y_off = jnp.einsum('bclgn,cbgrpn,bgrcl->bclgrp', Cc, prev, jnp.exp(a_cs))
    return (y_diag + y_off).reshape(b, l, g, r, p)


def _mixer(h, l, p):
    b, s, _ = h.shape
    proj = h @ p['w_in'][l]
    q, k, v, z, xbc, dt, gates = jnp.split(proj, SPLITS, axis=-1)
    q = q.reshape(b, s, ATT_HEADS, 2, ATT_QK_DIM)
    k = k.reshape(b, s, ATT_HEADS, 2, ATT_QK_DIM)
    v = v.reshape(b, s, ATT_HEADS, ATT_V_DIM)
    lam_init = 0.8 - 0.6 * math.exp(-0.3 * l)
    lam = (jnp.exp(jnp.sum(p['lam_q1'][l].astype(jnp.float32) * p['lam_k1'][l].astype(jnp.float32)))
           - jnp.exp(jnp.sum(p['lam_q2'][l].astype(jnp.float32) * p['lam_k2'][l].astype(jnp.float32)))
           + lam_init)
    o_att = _diff_attention(q, k, v, p['rel_bias'], lam)
    o_att = _rmsnorm(o_att, p['attn_subln'][l]) * (1.0 - lam_init)
    o_att = o_att.reshape(b, s, V_COLS) @ p['w_attn_o'][l]
    xbc = jax.nn.silu(_dwconv(xbc, p['conv_w'][l], p['conv_b'][l])).astype(jnp.float32)
    xs, Bm, Cm = jnp.split(xbc, [D_INNER, D_INNER + SSM_GROUPS * D_STATE], axis=-1)
    xs = xs.reshape(b, s, SSM_GROUPS, HEADS_PER_GROUP, SSM_HEADDIM)
    Bm = Bm.reshape(b, s, SSM_GROUPS, D_STATE)
    Cm = Cm.reshape(b, s, SSM_GROUPS, D_STATE)
    dt = jax.nn.softplus(dt.astype(jnp.float32).reshape(b, s, 2, SSM_HEADS)
                         + p['dt_bias'][l].astype(jnp.float32))
    dt = dt.reshape(b, s, 2, SSM_GROUPS, HEADS_PER_GROUP)
    A = -jnp.exp(p['a_log'][l].astype(jnp.float32)).reshape(2, SSM_GROUPS, HEADS_PER_GROUP)
    Dk = p['d_skip'][l].astype(jnp.float32).reshape(2, SSM_GROUPS, HEADS_PER_GROUP)
    flip = lambda t: jnp.flip(t, axis=1)
    y_f = _ssd(xs, dt[:, :, 0], A[0], Bm, Cm)
    y_b = flip(_ssd(flip(xs), flip(dt[:, :, 1]), A[1], flip(Bm), flip(Cm)))
    y = y_f + y_b + (Dk[0] + Dk[1])[..., None] * xs
    y = y.reshape(b, s, D_INNER) * jax.nn.silu(z.astype(jnp.float32))
    y = _rmsnorm(y.reshape(b, s, SSM_GROUPS, D_INNER // SSM_GROUPS),
                 p['ssm_norm'][l].reshape(SSM_GROUPS, D_INNER // SSM_GROUPS))
    o_ssm = y.reshape(b, s, D_INNER).astype(h.dtype) @ p['w_ssm_o'][l]
    g = jax.nn.sigmoid(gates.astype(jnp.float32)).reshape(b, s, 2, D_MODEL)
    merged = (g[:, :, 0] * o_att.astype(jnp.float32) + g[:, :, 1] * o_ssm.astype(jnp.float32)).astype(h.dtype)
    return merged @ p['w_out'][l]


def _cross_attn(h, mem, l, p):
    b, s, _ = h.shape
    m = _rmsnorm(mem, p['norm_mem'][l])
    q = (h @ p['w_cq'][l]).reshape(b, s, CROSS_HEADS, CROSS_DIM)
    kv = (m @ p['w_ckv'][l]).reshape(b, m.shape[1], 2, CROSS_HEADS, CROSS_DIM)
    sc = jnp.einsum('bqhd,bkhd->bhqk', q, kv[:, :, 0]).astype(jnp.float32) * (CROSS_DIM ** -0.5)
    pr = jax.nn.softmax(sc, axis=-1).astype(h.dtype)
    o = jnp.einsum('bhqk,bkhd->bqhd', pr, kv[:, :, 1]).reshape(b, s, D_MODEL)
    return o @ p['w_co'][l]


def _peer(h, l, p):
    b, s, d = h.shape
    nb = (b * s) // TOK_BLOCK
    xb = h.reshape(nb, TOK_BLOCK, d)
    w_pq = p['w_pq'][l]
    sub_keys = p['sub_keys'][l]
    u_tab = p['expert_u'][l]
    v_tab = p['expert_v'][l]

    def block(xt):
        q = (xt @ w_pq).reshape(TOK_BLOCK, PEER_HEADS, 2, D_KEY_HALF)
        sc = jnp.einsum('thcd,hcnd->thcn', q, sub_keys).astype(jnp.float32)
        v1, i1 = lax.top_k(sc[:, :, 0], PEER_TOPK)
        v2, i2 = lax.top_k(sc[:, :, 1], PEER_TOPK)
        cand = (v1[..., :, None] + v2[..., None, :]).reshape(TOK_BLOCK, PEER_HEADS, PEER_TOPK * PEER_TOPK)
        top, ci = lax.top_k(cand, PEER_TOPK)
        e1 = jnp.take_along_axis(i1, ci // PEER_TOPK, axis=-1)
        e2 = jnp.take_along_axis(i2, ci % PEER_TOPK, axis=-1)
        idx = (e1 * N_KEYS + e2).reshape(TOK_BLOCK, PEER_HEADS * PEER_TOPK)
        gate = jax.nn.softmax(top, axis=-1).reshape(TOK_BLOCK, PEER_HEADS * PEER_TOPK)
        act = jax.nn.gelu(jnp.einsum('tkd,td->tk', u_tab[idx], xt).astype(jnp.float32))
        coeff = (gate * act).astype(xt.dtype)
        return jnp.einsum('tk,tkd->td', coeff, v_tab[idx])

    return lax.map(block, xb).reshape(b, s, d)


def _trunk(x, mem, p):
    for l in range(DEPTH):
        x = x + _mixer(_rmsnorm(x, p['norm_mix'][l]), l, p)
        x = x + _cross_attn(_rmsnorm(x, p['norm_cross'][l]), mem, l, p)
        x = x + _peer(_rmsnorm(x, p['norm_ffn'][l]), l, p)
    return _rmsnorm(x, p['norm_final'])


def setup_inputs(seed: int = 0) -> dict:
    key = jax.random.key(seed)
    ks = list(jax.random.split(key, 40))
    f32 = jnp.float32

    def nrm(i, shape, std):
        return std * jax.random.normal(ks[i], shape, f32)

    def gain(i, shape):
        return 1.0 + 0.05 * jax.random.normal(ks[i], shape, f32)

    u = jax.random.uniform(ks[30], (DEPTH, 2, SSM_HEADS), f32)
    dt0 = jnp.maximum(jnp.exp(u * (math.log(0.1) - math.log(0.001)) + math.log(0.001)), 1e-4)
    dt_bias = dt0 + jnp.log(-jnp.expm1(-dt0))
    a_log = jnp.log(jax.random.uniform(ks[31], (DEPTH, 2, SSM_HEADS), f32, 1.0, 16.0))
    return {
        'x_prompt': nrm(0, (BATCH, SEQ, D_MODEL), 1.0),
        'x_sample': nrm(1, (DEC_BATCH, DEC_SEQ, D_MODEL), 1.0),
        'mem_prompt': nrm(2, (BATCH, N_MEM, D_MODEL), 1.0),
        'mem_sample': nrm(3, (DEC_BATCH, N_MEM, D_MODEL), 1.0),
        'norm_mix': gain(4, (DEPTH, D_MODEL)),
        'w_in': nrm(5, (DEPTH, D_MODEL, IN_COLS), D_MODEL ** -0.5),
        'lam_q1': nrm(6, (DEPTH, ATT_QK_DIM), 0.1),
        'lam_k1': nrm(7, (DEPTH, ATT_QK_DIM), 0.1),
        'lam_q2': nrm(8, (DEPTH, ATT_QK_DIM), 0.1),
        'lam_k2': nrm(9, (DEPTH, ATT_QK_DIM), 0.1),
        'rel_bias': nrm(10, (N_BUCKETS, ATT_HEADS), 0.3),
        'attn_subln': gain(11, (DEPTH, ATT_V_DIM)),
        'w_attn_o': nrm(12, (DEPTH, V_COLS, D_MODEL), V_COLS ** -0.5),
        'conv_w': nrm(13, (DEPTH, D_CONV, CONV_DIM), D_CONV ** -0.5),
        'conv_b': nrm(14, (DEPTH, CONV_DIM), 0.02),
        'a_log': a_log,
        'dt_bias': dt_bias,
        'd_skip': 0.5 + 0.1 * jax.random.normal(ks[15], (DEPTH, 2, SSM_HEADS), f32),
        'ssm_norm': gain(16, (DEPTH, D_INNER)),
        'w_ssm_o': nrm(17, (DEPTH, D_INNER, D_MODEL), D_INNER ** -0.5),
        'w_out': nrm(18, (DEPTH, D_MODEL, D_MODEL), D_MODEL ** -0.5),
        'norm_cross': gain(19, (DEPTH, D_MODEL)),
        'norm_mem': gain(20, (DEPTH, D_MODEL)),
        'w_cq': nrm(21, (DEPTH, D_MODEL, D_MODEL), D_MODEL ** -0.5),
        'w_ckv': nrm(22, (DEPTH, D_MODEL, 2 * D_MODEL), D_MODEL ** -0.5),
        'w_co': nrm(23, (DEPTH, D_MODEL, D_MODEL), D_MODEL ** -0.5),
        'norm_ffn': gain(24, (DEPTH, D_MODEL)),
        'w_pq': nrm(25, (DEPTH, D_MODEL, PEER_HEADS * D_KEY), D_MODEL ** -0.5),
        'sub_keys': nrm(26, (DEPTH, PEER_HEADS, 2, N_KEYS, D_KEY_HALF), D_KEY_HALF ** -0.5),
        'expert_u': nrm(27, (DEPTH, N_EXPERTS, D_MODEL), D_MODEL ** -0.5),
        'expert_v': nrm(28, (DEPTH, N_EXPERTS, D_MODEL), PEER_HEADS ** -0.5),
        'norm_final': gain(29, (D_MODEL,)),
    }


def reference(x_prompt, x_sample, mem_prompt, mem_sample, norm_mix, w_in, lam_q1, lam_k1,
              lam_q2, lam_k2, rel_bias, attn_subln, w_attn_o, conv_w, conv_b, a_log, dt_bias,
              d_skip, ssm_norm, w_ssm_o, w_out, norm_cross, norm_mem, w_cq, w_ckv, w_co,
              norm_ffn, w_pq, sub_keys, expert_u, expert_v, norm_final):
    p = dict(norm_mix=norm_mix, w_in=w_in, lam_q1=lam_q1, lam_k1=lam_k1, lam_q2=lam_q2,
             lam_k2=lam_k2, rel_bias=rel_bias, attn_subln=attn_subln, w_attn_o=w_attn_o,
             conv_w=conv_w, conv_b=conv_b, a_log=a_log, dt_bias=dt_bias, d_skip=d_skip,
             ssm_norm=ssm_norm, w_ssm_o=w_ssm_o, w_out=w_out, norm_cross=norm_cross,
             norm_mem=norm_mem, w_cq=w_cq, w_ckv=w_ckv, w_co=w_co, norm_ffn=norm_ffn,
             w_pq=w_pq, sub_keys=sub_keys, expert_u=expert_u, expert_v=expert_v,
             norm_final=norm_final)
    y_prompt = _trunk(x_prompt, mem_prompt, p)
    y_sample = _trunk(x_sample, mem_sample, p)
    return (y_prompt, y_sample)
```

```python
import functools
import math

import jax
import jax.numpy as jnp
import numpy as np
from jax import lax
from jax.experimental import pallas as pl
from jax.experimental.pallas import tpu as pltpu

F32 = jnp.float32
BF16 = jnp.bfloat16

D_MODEL = 2048
ATT_HEADS = 8
ATT_V_DIM = 128
ATT_QK_DIM = 64
N_BUCKETS = 32
MAX_DISTANCE = 128
D_INNER = 1024
SSM_HEADDIM = 64
SSM_HEADS = 16
SSM_GROUPS = 2
HEADS_PER_GROUP = 8
D_STATE = 128
D_CONV = 5
CONV_PAD = 2
CONV_DIM = D_INNER + 2 * SSM_GROUPS * D_STATE
SSM_CHUNK = 128
CROSS_HEADS = 4
CROSS_DIM = 512
PEER_HEADS = 8
N_KEYS = 128
N_EXPERTS = N_KEYS * N_KEYS
PEER_TOPK = 16
D_KEY = 128
D_KEY_HALF = 64
EPS = 1e-6

Q_COLS = 1024
V_COLS = 1024
DT_COLS = 32
LANES = 128
HALO = 8
ATTN_TILE = 512
VMEM_LIMIT = 56 * 1024 * 1024


def _cparams(sem):
    return pltpu.CompilerParams(dimension_semantics=sem, vmem_limit_bytes=VMEM_LIMIT)


def _pick(n, pref):
    t = min(pref, n)
    while n % t:
        t //= 2
    return t


def _rmsnorm_kernel(x_ref, g_ref, o_ref):
    x = x_ref[...]
    y = x * lax.rsqrt(jnp.mean(x * x, axis=-1, keepdims=True) + EPS)
    o_ref[...] = (y * g_ref[...]).astype(o_ref.dtype)


def _rmsnorm_bf16(x, g):
    m, d = x.shape
    tm = _pick(m, 512)
    return pl.pallas_call(
        _rmsnorm_kernel,
        grid=(m // tm,),
        in_specs=[pl.BlockSpec((tm, d), lambda i: (i, 0)), pl.BlockSpec((1, d), lambda i: (0, 0))],
        out_specs=pl.BlockSpec((tm, d), lambda i: (i, 0)),
        out_shape=jax.ShapeDtypeStruct((m, d), BF16),
        compiler_params=_cparams(("parallel",)),
        name="rmsnorm_bf16",
    )(x, g.reshape(1, d).astype(F32))


def _mm_kernel(x_ref, w_ref, o_ref):
    o_ref[...] = jnp.dot(x_ref[...], w_ref[...], preferred_element_type=F32).astype(o_ref.dtype)


def _mm_residual_kernel(x_ref, w_ref, r_ref, o_ref):
    acc = jnp.dot(x_ref[...], w_ref[...], preferred_element_type=F32)
    o_ref[...] = (r_ref[...] + acc).astype(o_ref.dtype)


def _mm_merge_kernel(x_ref, w_ref, t_ref, g0_ref, g1_ref, o_ref):
    acc = jnp.dot(x_ref[...], w_ref[...], preferred_element_type=F32)
    g0 = jax.nn.sigmoid(g0_ref[...])
    g1 = jax.nn.sigmoid(g1_ref[...])
    o_ref[...] = (g0 * t_ref[...] + g1 * acc).astype(o_ref.dtype)


def _mm(x, w, out_dtype, *, residual=None, merge=None, tm=512, tn=1024, name="mm"):
    m, k = x.shape
    n = w.shape[1]
    tm = _pick(m, tm)
    tn = _pick(n, tn)
    in_specs = [pl.BlockSpec((tm, k), lambda i, j: (i, 0)), pl.BlockSpec((k, tn), lambda i, j: (0, j))]
    args = [x, w]
    kern = _mm_kernel
    if residual is not None:
        kern = _mm_residual_kernel
        in_specs.append(pl.BlockSpec((tm, tn), lambda i, j: (i, j)))
        args.append(residual)
    if merge is not None:
        kern = _mm_merge_kernel
        t_att, gates = merge
        nb = n // tn
        in_specs += [
            pl.BlockSpec((tm, tn), lambda i, j: (i, j)),
            pl.BlockSpec((tm, tn), lambda i, j: (i, j)),
            pl.BlockSpec((tm, tn), lambda i, j: (i, j + nb)),
        ]
        args += [t_att, gates, gates]
    return pl.pallas_call(
        kern,
        grid=(m // tm, n // tn),
        in_specs=in_specs,
        out_specs=pl.BlockSpec((tm, tn), lambda i, j: (i, j)),
        out_shape=jax.ShapeDtypeStruct((m, n), out_dtype),
        compiler_params=_cparams(("parallel", "parallel")),
        name=name,
    )(*args)


def _t5_bucket(rel):
    half = N_BUCKETS // 2
    exact = half // 2
    n = jnp.abs(rel)
    far = exact + (
        jnp.log(jnp.maximum(n, 1).astype(F32) / exact) / math.log(MAX_DISTANCE / exact) * (half - exact)
    ).astype(jnp.int32)
    far = jnp.minimum(far, half - 1)
    return jnp.where(rel > 0, half, 0) + jnp.where(n < exact, n, far)


def _attn_kernel(lam_ref, far_ref, q_ref, k_ref, v_ref, bias_ref, g_ref, o_ref, m_scr, l_scr, acc_scr, *, t, nk,
                 out_scale):
    h = pl.program_id(1)
    qi = pl.program_id(2)
    q = q_ref[...].astype(F32) * (ATT_QK_DIM ** -0.5)
    lane = lax.broadcasted_iota(jnp.int32, q.shape, 1)
    qz = (jnp.where(lane < ATT_QK_DIM, q, 0.0).astype(BF16), jnp.where(lane >= ATT_QK_DIM, q, 0.0).astype(BF16))

    m_scr[...] = jnp.full(m_scr.shape, -jnp.inf, F32)
    l_scr[...] = jnp.zeros(l_scr.shape, F32)
    acc_scr[...] = jnp.zeros(acc_scr.shape, F32)

    def chunk(kc, add_bias):
        start = pl.multiple_of(kc * t, t)
        kblk = k_ref[pl.ds(start, t), :]
        vblk = v_ref[pl.ds(start, t), :]
        for c in range(2):
            s = lax.dot_general(qz[c], kblk, (((1,), (1,)), ((), ())), preferred_element_type=F32)
            s = add_bias(s)
            m_old = m_scr[c]
            m_new = jnp.maximum(m_old, jnp.max(s, axis=-1, keepdims=True))
            alpha = jnp.exp(m_old - m_new)
            p = jnp.exp(s - m_new)
            l_scr[c] = alpha * l_scr[c] + jnp.sum(p, axis=-1, keepdims=True)
            acc_scr[c] = alpha * acc_scr[c] + jnp.dot(p.astype(BF16), vblk, preferred_element_type=F32)
            m_scr[c] = m_new

    lo_end = jnp.maximum(qi - 1, 0)
    near_end = jnp.minimum(qi + 2, nk)
    far_lo = far_ref[h, 0]
    far_hi = far_ref[h, 1]

    def body_lo(kc, carry):
        chunk(kc, lambda s: s + far_lo)
        return carry

    def body_near(kc, carry):
        chunk(kc, lambda s: s + bias_ref[kc - qi + 1])
        return carry

    def body_hi(kc, carry):
        chunk(kc, lambda s: s + far_hi)
        return carry

    lax.fori_loop(0, lo_end, body_lo, 0)
    lax.fori_loop(lo_end, near_end, body_near, 0)
    lax.fori_loop(near_end, nk, body_hi, 0)

    lam = lam_ref[0]
    o = acc_scr[0] / l_scr[0] - lam * (acc_scr[1] / l_scr[1])
    y = o * lax.rsqrt(jnp.mean(o * o, axis=-1, keepdims=True) + EPS)
    o_ref[...] = ((y * g_ref[...]) * out_scale).astype(o_ref.dtype)


def _diff_attention(qkv, b, s, rel_bias, lam, subln, out_scale, t):
    nk = s // t
    hq = ATT_HEADS
    r = jnp.arange(t, dtype=jnp.int32)
    rel = (jnp.arange(3, dtype=jnp.int32)[:, None, None] - 1) * t + r[None, None, :] - r[None, :, None]
    bias_tiles = rel_bias.astype(F32)[_t5_bucket(rel)].transpose(3, 0, 1, 2)
    half = N_BUCKETS // 2
    far = jnp.stack([rel_bias[half - 1], rel_bias[N_BUCKETS - 1]], axis=-1).astype(F32)
    kern = functools.partial(_attn_kernel, t=t, nk=nk, out_scale=out_scale)
    return pl.pallas_call(
        kern,
        grid=(b, hq, nk),
        in_specs=[
            pl.BlockSpec(memory_space=pltpu.SMEM),
            pl.BlockSpec(memory_space=pltpu.SMEM),
            pl.BlockSpec((t, LANES), lambda bi, h, qi: (bi * nk + qi, h)),
            pl.BlockSpec((s, LANES), lambda bi, h, qi: (bi, hq + h)),
            pl.BlockSpec((s, LANES), lambda bi, h, qi: (bi, 2 * hq + h)),
            pl.BlockSpec((None, 3, t, t), lambda bi, h, qi: (h, 0, 0, 0)),
            pl.BlockSpec((1, LANES), lambda bi, h, qi: (0, 0)),
        ],
        out_specs=pl.BlockSpec((t, LANES), lambda bi, h, qi: (bi * nk + qi, h)),
        out_shape=jax.ShapeDtypeStruct((b * s, V_COLS), BF16),
        scratch_shapes=[
            pltpu.VMEM((2, t, 1), F32),
            pltpu.VMEM((2, t, 1), F32),
            pltpu.VMEM((2, t, LANES), F32),
        ],
        compiler_params=_cparams(("parallel", "parallel", "parallel")),
        name="diff_attention",
    )(lam.reshape(1).astype(F32), far, qkv, qkv, qkv, bias_tiles, subln.reshape(1, LANES).astype(F32))


def _conv_kernel(prev_ref, cur_ref, next_ref, w_ref, b_ref, o_ref, ext_scr, *, tb, ns):
    si = pl.program_id(1)
    ext_scr[0:HALO, :] = jnp.where(si > 0, prev_ref[...], 0.0)
    ext_scr[HALO:HALO + tb, :] = cur_ref[...]
    ext_scr[HALO + tb:2 * HALO + tb, :] = jnp.where(si < ns - 1, next_ref[...], 0.0)
    y = b_ref[...]
    for j in range(D_CONV):
        off = HALO - CONV_PAD + j
        y = y + w_ref[j:j + 1, :] * ext_scr[off:off + tb, :]
    o_ref[...] = y * jax.nn.sigmoid(y)


def _conv_silu(xbc, b, s, conv_w, conv_b, tb=512, tc=512):
    tb = _pick(s, tb)
    ns = s // tb
    hb = tb // HALO
    nc = CONV_DIM // tc
    kern = functools.partial(_conv_kernel, tb=tb, ns=ns)
    return pl.pallas_call(
        kern,
        grid=(b, ns, nc),
        in_specs=[
            pl.BlockSpec((HALO, tc), lambda bi, si, ci: (jnp.maximum((bi * ns + si) * hb - 1, 0), ci)),
            pl.BlockSpec((tb, tc), lambda bi, si, ci: (bi * ns + si, ci)),
            pl.BlockSpec((HALO, tc), lambda bi, si, ci: (jnp.minimum((bi * ns + si + 1) * hb, b * ns * hb - 1), ci)),
            pl.BlockSpec((D_CONV, tc), lambda bi, si, ci: (0, ci)),
            pl.BlockSpec((1, tc), lambda bi, si, ci: (0, ci)),
        ],
        out_specs=pl.BlockSpec((tb, tc), lambda bi, si, ci: (bi * ns + si, ci)),
        out_shape=jax.ShapeDtypeStruct(xbc.shape, F32),
        scratch_shapes=[pltpu.VMEM((tb + 2 * HALO, tc), F32)],
        compiler_params=_cparams(("parallel", "parallel", "parallel")),
        name="conv_silu",
    )(xbc, xbc, xbc, conv_w.astype(F32), conv_b.reshape(1, CONV_DIM).astype(F32))


def _softplus(x):
    return jnp.maximum(x, 0.0) + jnp.log1p(jnp.exp(-jnp.abs(x)))


def _ssd_kernel(*refs, reverse):
    if reverse:
        x_ref, dt_ref, dtb_ref, a_ref, yf_ref, z_ref, dsk_ref, gn_ref, o_ref, st_scr = refs
    else:
        x_ref, dt_ref, dtb_ref, a_ref, o_ref, st_scr = refs
    L = SSM_CHUNK
    d = 1 if reverse else 0

    @pl.when(pl.program_id(1) == 0)
    def _():
        st_scr[...] = jnp.zeros(st_scr.shape, F32)

    dt = _softplus(dt_ref[...] + dtb_ref[...])
    a = dt * a_ref[...]
    row = lax.broadcasted_iota(jnp.int32, (L, L), 0)
    col = lax.broadcasted_iota(jnp.int32, (L, L), 1)
    tril = (row >= col).astype(F32)
    acs = jnp.dot(tril, a, preferred_element_type=F32, precision=lax.Precision.HIGHEST)
    ecs = acs - a if reverse else acs
    ecs_t = ecs.T
    keep = (col >= row) if reverse else (row >= col)
    total = acs[L - 1:L, :]

    ys = []
    for g in range(SSM_GROUPS):
        bg = x_ref[:, D_INNER + g * D_STATE:D_INNER + (g + 1) * D_STATE]
        cg = x_ref[:, D_INNER + SSM_GROUPS * D_STATE + g * D_STATE:D_INNER + SSM_GROUPS * D_STATE + (g + 1) * D_STATE]
        cg16 = cg.astype(BF16)
        cb = lax.dot_general(cg16, bg.astype(BF16), (((1,), (1,)), ((), ())), preferred_element_type=F32)
        for r in range(HEADS_PER_GROUP):
            hd = g * HEADS_PER_GROUP + r
            c = d * SSM_HEADS + hd
            e_col = ecs[:, c:c + 1]
            e_row = ecs_t[c:c + 1, :]
            tot = total[:, c:c + 1]
            if reverse:
                diff = e_row - e_col
                out_decay = jnp.exp(tot - e_col)
                in_decay = jnp.exp(e_col)
            else:
                diff = e_col - e_row
                out_decay = jnp.exp(e_col)
                in_decay = jnp.exp(tot - e_col)
            seg = jnp.exp(jnp.where(keep, diff, -jnp.inf))
            xdt = (x_ref[:, hd * SSM_HEADDIM:(hd + 1) * SSM_HEADDIM] * dt[:, c:c + 1]).astype(BF16)
            y = jnp.dot((seg * cb).astype(BF16), xdt, preferred_element_type=F32)
            st = st_scr[hd]
            y = y + jnp.dot(cg16, st.astype(BF16), preferred_element_type=F32) * out_decay
            upd = lax.dot_general((bg * in_decay).astype(BF16), xdt, (((0,), (0,)), ((), ())),
                                  preferred_element_type=F32)
            st_scr[hd] = st * jnp.exp(tot) + upd
            ys.append(y)
    y = jnp.concatenate(ys, axis=-1)
    if not reverse:
        o_ref[...] = y
        return
    xs = x_ref[:, 0:D_INNER]
    y = y + yf_ref[...] + dsk_ref[...] * xs
    z = z_ref[...]
    y = y * (z * jax.nn.sigmoid(z))
    gw = D_INNER // SSM_GROUPS
    outs = []
    for g in range(SSM_GROUPS):
        yg = y[:, g * gw:(g + 1) * gw]
        yn = yg * lax.rsqrt(jnp.mean(yg * yg, axis=-1, keepdims=True) + EPS)
        outs.append(yn * gn_ref[:, g * gw:(g + 1) * gw])
    o_ref[...] = jnp.concatenate(outs, axis=-1).astype(o_ref.dtype)


def _ssd(xbc_act, dt_raw, z, b, s, dt_bias, a_log, d_skip, ssm_norm):
    L = SSM_CHUNK
    nc = s // L
    pad = LANES - DT_COLS
    dtb = jnp.pad(dt_bias.astype(F32).reshape(1, DT_COLS), ((0, 0), (0, pad)))
    a_neg = jnp.pad(-jnp.exp(a_log.astype(F32)).reshape(1, DT_COLS), ((0, 0), (0, pad)))
    dsk = d_skip.astype(F32)
    dsk = jnp.repeat(dsk[0] + dsk[1], SSM_HEADDIM).reshape(1, D_INNER)
    st_shape = pltpu.VMEM((SSM_HEADS, D_STATE, SSM_HEADDIM), F32)
    small = lambda w: pl.BlockSpec((1, w), lambda bi, ci: (0, 0))

    fwd_map = lambda bi, ci: (bi * nc + ci, 0)
    y_f = pl.pallas_call(
        functools.partial(_ssd_kernel, reverse=False),
        grid=(b, nc),
        in_specs=[pl.BlockSpec((L, CONV_DIM), fwd_map), pl.BlockSpec((L, LANES), fwd_map), small(LANES), small(LANES)],
        out_specs=pl.BlockSpec((L, D_INNER), fwd_map),
        out_shape=jax.ShapeDtypeStruct((b * s, D_INNER), F32),
        scratch_shapes=[st_shape],
        compiler_params=_cparams(("parallel", "arbitrary")),
        name="ssd_fwd",
    )(xbc_act, dt_raw, dtb, a_neg)

    bwd_map = lambda bi, ci: (bi * nc + (nc - 1 - ci), 0)
    return pl.pallas_call(
        functools.partial(_ssd_kernel, reverse=True),
        grid=(b, nc),
        in_specs=[
            pl.BlockSpec((L, CONV_DIM), bwd_map),
            pl.BlockSpec((L, LANES), bwd_map),
            small(LANES),
            small(LANES),
            pl.BlockSpec((L, D_INNER), bwd_map),
            pl.BlockSpec((L, D_INNER), bwd_map),
            small(D_INNER),
            small(D_INNER),
        ],
        out_specs=pl.BlockSpec((L, D_INNER), bwd_map),
        out_shape=jax.ShapeDtypeStruct((b * s, D_INNER), BF16),
        scratch_shapes=[st_shape],
        compiler_params=_cparams(("parallel", "arbitrary")),
        name="ssd_bwd",
    )(xbc_act, dt_raw, dtb, a_neg, y_f, z, dsk, ssm_norm.astype(F32).reshape(1, D_INNER))


def _cross_kernel(q_ref, k_ref, v_ref, o_ref):
    outs = []
    for h in range(CROSS_HEADS):
        sl = slice(h * CROSS_DIM, (h + 1) * CROSS_DIM)
        s = lax.dot_general(q_ref[:, sl], k_ref[:, sl], (((1,), (1,)), ((), ())), preferred_element_type=F32)
        s = s * (CROSS_DIM ** -0.5)
        p = jnp.exp(s - jnp.max(s, axis=-1, keepdims=True))
        p = p / jnp.sum(p, axis=-1, keepdims=True)
        outs.append(jnp.dot(p.astype(BF16), v_ref[:, sl], preferred_element_type=F32))
    o_ref[...] = jnp.concatenate(outs, axis=-1).astype(o_ref.dtype)


def _cross_core(q, kv, b, s, n_mem, tq=512):
    tq = _pick(s, tq)
    nq = s // tq
    return pl.pallas_call(
        _cross_kernel,
        grid=(b, nq),
        in_specs=[
            pl.BlockSpec((tq, D_MODEL), lambda bi, qi: (bi * nq + qi, 0)),
            pl.BlockSpec((n_mem, D_MODEL), lambda bi, qi: (bi, 0)),
            pl.BlockSpec((n_mem, D_MODEL), lambda bi, qi: (bi, 1)),
        ],
        out_specs=pl.BlockSpec((tq, D_MODEL), lambda bi, qi: (bi * nq + qi, 0)),
        out_shape=jax.ShapeDtypeStruct((b * s, D_MODEL), BF16),
        compiler_params=_cparams(("parallel", "parallel")),
        name="cross_attention",
    )(q, kv, kv)


def _top16(s, vals_scr, idx_scr, payload=None):
    n = s.shape[0]
    pos = lax.broadcasted_iota(jnp.int32, s.shape, 0).astype(F32)
    for k in range(PEER_TOPK):
        m = jnp.max(s, axis=0, keepdims=True)
        first = jnp.min(jnp.where(s == m, pos, float(n)), axis=0, keepdims=True)
        sel = pos == first
        vals_scr[k:k + 1, :] = m
        if payload is None:
            idx_scr[k:k + 1, :] = first
        else:
            idx_scr[k:k + 1, :] = jnp.max(jnp.where(sel, payload, -1.0), axis=0, keepdims=True)
        s = jnp.where(sel, -jnp.inf, s)


def _topk_kernel(q_ref, keys_ref, idx_ref, gate_ref, v1_scr, i1_scr, v2_scr, i2_scr, tv_scr, ti_scr):
    for h in range(PEER_HEADS):
        qh = q_ref[:, h * D_KEY:(h + 1) * D_KEY]
        sc = lax.dot_general(keys_ref[h], qh, (((1,), (1,)), ((), ())), preferred_element_type=F32)
        _top16(sc[0:N_KEYS], v1_scr, i1_scr)
        _top16(sc[N_KEYS:2 * N_KEYS], v2_scr, i2_scr)
        v2 = v2_scr[...]
        i2 = i2_scr[...]
        cand = jnp.concatenate([v1_scr[i:i + 1, :] + v2 for i in range(PEER_TOPK)], axis=0)
        eidx = jnp.concatenate([i1_scr[i:i + 1, :] * N_KEYS + i2 for i in range(PEER_TOPK)], axis=0)
        _top16(cand, tv_scr, ti_scr, payload=eidx)
        top = tv_scr[...]
        e = jnp.exp(top - top[0:1, :])
        gate_ref[h * PEER_TOPK:(h + 1) * PEER_TOPK, :] = e / jnp.sum(e, axis=0, keepdims=True)
        idx_ref[h * PEER_TOPK:(h + 1) * PEER_TOPK, :] = ti_scr[...]


def _peer_topk(pq, keys_bd, tb=128):
    tt = pq.shape[0]
    nhk = PEER_HEADS * PEER_TOPK
    sc16 = lambda: pltpu.VMEM((PEER_TOPK, tb), F32)
    return pl.pallas_call(
        _topk_kernel,
        grid=(tt // tb,),
        in_specs=[
            pl.BlockSpec((tb, PEER_HEADS * D_KEY), lambda i: (i, 0)),
            pl.BlockSpec((PEER_HEADS, 2 * N_KEYS, D_KEY), lambda i: (0, 0, 0)),
        ],
        out_specs=[pl.BlockSpec((nhk, tb), lambda i: (0, i)), pl.BlockSpec((nhk, tb), lambda i: (0, i))],
        out_shape=[jax.ShapeDtypeStruct((nhk, tt), F32), jax.ShapeDtypeStruct((nhk, tt), F32)],
        scratch_shapes=[sc16() for _ in range(6)],
        compiler_params=_cparams(("parallel",)),
        name="peer_topk",
    )(pq, keys_bd)


def _gate_kernel(idx_ref, gate_ref, o_ref, idx_scr, gate_scr, *, tg):
    idx_scr[...] = idx_ref[...].T
    gate_scr[...] = gate_ref[...].T
    pos = lax.broadcasted_iota(jnp.int32, (N_KEYS, PEER_HEADS * PEER_TOPK), 0)

    def body(t, carry):
        e = idx_scr[pl.ds(t, 1), :].astype(jnp.int32)
        g = gate_scr[pl.ds(t, 1), :]
        a_t = jnp.where(pos == (e >> 7), 1.0, 0.0).astype(BF16)
        b_t = jnp.where(pos == (e & (N_KEYS - 1)), g, 0.0).astype(BF16)
        gt = lax.dot_general(a_t, b_t, (((1,), (1,)), ((), ())), preferred_element_type=F32)
        o_ref[t] = gt.astype(o_ref.dtype)
        return carry

    lax.fori_loop(0, tg, body, 0)


def _peer_gates(idx, gate, tg=128):
    nhk, tt = idx.shape
    return pl.pallas_call(
        functools.partial(_gate_kernel, tg=tg),
        grid=(tt // tg,),
        in_specs=[pl.BlockSpec((nhk, tg), lambda i: (0, i)), pl.BlockSpec((nhk, tg), lambda i: (0, i))],
        out_specs=pl.BlockSpec((tg, N_KEYS, N_KEYS), lambda i: (i, 0, 0)),
        out_shape=jax.ShapeDtypeStruct((tt, N_KEYS, N_KEYS), BF16),
        scratch_shapes=[pltpu.VMEM((tg, nhk), F32), pltpu.VMEM((tg, nhk), F32)],
        compiler_params=_cparams(("parallel",)),
        name="peer_gates",
    )(idx, gate)


def _gelu_tanh(x):
    return x * (0.5 * (1.0 + jnp.tanh(math.sqrt(2.0 / math.pi) * (x + 0.044715 * (x * x * x)))))


def _ffn_kernel(h_ref, u_ref, v_ref, g_ref, x_ref, nf_ref, o_ref, acc_scr):
    j = pl.program_id(1)

    @pl.when(j == 0)
    def _():
        acc_scr[...] = jnp.zeros(acc_scr.shape, F32)

    s = lax.dot_general(h_ref[...], u_ref[...], (((1,), (1,)), ((), ())), preferred_element_type=F32)
    coeff = (g_ref[...].astype(F32) * _gelu_tanh(s)).astype(BF16)
    acc_scr[...] += jnp.dot(coeff, v_ref[...], preferred_element_type=F32)

    @pl.when(j == pl.num_programs(1) - 1)
    def _():
        x = x_ref[...] + acc_scr[...]
        y = x * lax.rsqrt(jnp.mean(x * x, axis=-1, keepdims=True) + EPS)
        o_ref[...] = y * nf_ref[...]


def _peer_ffn(hp, u16, v16, gmat, x_res, norm_final, tm=512, te=512):
    tt = hp.shape[0]
    tm = _pick(tt, tm)
    return pl.pallas_call(
        _ffn_kernel,
        grid=(tt // tm, N_EXPERTS // te),
        in_specs=[
            pl.BlockSpec((tm, D_MODEL), lambda i, j: (i, 0)),
            pl.BlockSpec((te, D_MODEL), lambda i, j: (j, 0)),
            pl.BlockSpec((te, D_MODEL), lambda i, j: (j, 0)),
            pl.BlockSpec((tm, te), lambda i, j: (i, j)),
            pl.BlockSpec((tm, D_MODEL), lambda i, j: (i, 0)),
            pl.BlockSpec((1, D_MODEL), lambda i, j: (0, 0)),
        ],
        out_specs=pl.BlockSpec((tm, D_MODEL), lambda i, j: (i, 0)),
        out_shape=jax.ShapeDtypeStruct((tt, D_MODEL), F32),
        scratch_shapes=[pltpu.VMEM((tm, D_MODEL), F32)],
        compiler_params=_cparams(("parallel", "arbitrary")),
        name="peer_ffn",
    )(hp, u16, v16, gmat, x_res, norm_final.reshape(1, D_MODEL).astype(F32))


def _prepare(p):
    w_in = p["w_in"][0]
    c = np.cumsum([0, 3 * Q_COLS, D_INNER, CONV_DIM, DT_COLS, 2 * D_MODEL])
    w = {}
    w["qkv"] = w_in[:, c[0]:c[1]].astype(BF16)
    w["z"] = w_in[:, c[1]:c[2]].astype(BF16)
    w["xbc"] = w_in[:, c[2]:c[3]].astype(BF16)
    w["dt"] = jnp.pad(w_in[:, c[3]:c[4]], ((0, 0), (0, LANES - DT_COLS))).astype(BF16)
    w["gates"] = w_in[:, c[4]:c[5]].astype(BF16)
    for name in ("w_attn_o", "w_ssm_o", "w_out", "w_cq", "w_ckv", "w_co", "w_pq", "expert_u", "expert_v"):
        w[name] = p[name][0].astype(BF16)
    sk = p["sub_keys"][0].astype(BF16)
    zero = jnp.zeros_like(sk[:, 0])
    w["keys_bd"] = jnp.concatenate(
        [jnp.concatenate([sk[:, 0], zero], axis=-1), jnp.concatenate([zero, sk[:, 1]], axis=-1)], axis=1
    )
    lam_init = 0.8 - 0.6 * math.exp(-0.3 * 0)
    w["lam"] = (
        jnp.exp(jnp.sum(p["lam_q1"][0].astype(F32) * p["lam_k1"][0].astype(F32)))
        - jnp.exp(jnp.sum(p["lam_q2"][0].astype(F32) * p["lam_k2"][0].astype(F32)))
        + lam_init
    )
    w["lam_init"] = lam_init
    return w


def _trunk(x, mem, p, w):
    b, s, d = x.shape
    n_mem = mem.shape[1]
    xf = x.reshape(b * s, d)

    h1 = _rmsnorm_bf16(xf, p["norm_mix"][0])
    qkv = _mm(h1, w["qkv"], BF16, name="mm_qkv")
    z = _mm(h1, w["z"], F32, name="mm_z")
    xbc = _mm(h1, w["xbc"], F32, tn=512, name="mm_xbc")
    dt_raw = _mm(h1, w["dt"], F32, name="mm_dt")
    gates = _mm(h1, w["gates"], F32, name="mm_gates")

    o_att = _diff_attention(qkv, b, s, p["rel_bias"], w["lam"], p["attn_subln"][0], 1.0 - w["lam_init"],
                            _pick(s, ATTN_TILE))
    xbc_act = _conv_silu(xbc, b, s, p["conv_w"][0], p["conv_b"][0])
    y_ssm = _ssd(xbc_act, dt_raw, z, b, s, p["dt_bias"][0], p["a_log"][0], p["d_skip"][0], p["ssm_norm"][0])

    t_att = _mm(o_att, w["w_attn_o"], F32, name="mm_attn_o")
    merged = _mm(y_ssm, w["w_ssm_o"], BF16, merge=(t_att, gates), name="mm_ssm_o_merge")
    x1 = _mm(merged, w["w_out"], F32, residual=xf, name="mm_out")

    hq = _rmsnorm_bf16(x1, p["norm_cross"][0])
    q = _mm(hq, w["w_cq"], BF16, name="mm_cq")
    mn = _rmsnorm_bf16(mem.reshape(b * n_mem, d), p["norm_mem"][0])
    kv = _mm(mn, w["w_ckv"], BF16, name="mm_ckv")
    oc = _cross_core(q, kv, b, s, n_mem)
    x2 = _mm(oc, w["w_co"], F32, residual=x1, name="mm_co")

    hp = _rmsnorm_bf16(x2, p["norm_ffn"][0])
    pq = _mm(hp, w["w_pq"], BF16, name="mm_pq")
    idx, gate = _peer_topk(pq, w["keys_bd"])
    gmat = _peer_gates(idx, gate).reshape(b * s, N_EXPERTS)
    y = _peer_ffn(hp, w["expert_u"], w["expert_v"], gmat, x2, p["norm_final"])
    return y.reshape(b, s, d)


def kernel(x_prompt, x_sample, mem_prompt, mem_sample, norm_mix, w_in, lam_q1, lam_k1, lam_q2, lam_k2, rel_bias, attn_subln, w_attn_o, conv_w, conv_b, a_log, dt_bias, d_skip, ssm_norm, w_ssm_o, w_out, norm_cross, norm_mem, w_cq, w_ckv, w_co, norm_ffn, w_pq, sub_keys, expert_u, expert_v, norm_final):
    p = dict(norm_mix=norm_mix, w_in=w_in, lam_q1=lam_q1, lam_k1=lam_k1, lam_q2=lam_q2, lam_k2=lam_k2,
             rel_bias=rel_bias, attn_subln=attn_subln, w_attn_o=w_attn_o, conv_w=conv_w, conv_b=conv_b,
             a_log=a_log, dt_bias=dt_bias, d_skip=d_skip, ssm_norm=ssm_norm, w_ssm_o=w_ssm_o, w_out=w_out,
             norm_cross=norm_cross, norm_mem=norm_mem, w_cq=w_cq, w_ckv=w_ckv, w_co=w_co, norm_ffn=norm_ffn,
             w_pq=w_pq, sub_keys=sub_keys, expert_u=expert_u, expert_v=expert_v, norm_final=norm_final)
    w = _prepare(p)
    y_prompt = _trunk(x_prompt, mem_prompt, p, w)
    y_sample = _trunk(x_sample, mem_sample, p, w)
    return (y_prompt, y_sample)
```

```python
import functools
import math

import jax
import jax.numpy as jnp
import numpy as np
from jax import lax
from jax.experimental import pallas as pl
from jax.experimental.pallas import tpu as pltpu

F32 = jnp.float32
BF16 = jnp.bfloat16

D_MODEL = 2048
ATT_HEADS = 8
ATT_V_DIM = 128
ATT_QK_DIM = 64
N_BUCKETS = 32
MAX_DISTANCE = 128
D_INNER = 1024
SSM_HEADDIM = 64
SSM_HEADS = 16
SSM_GROUPS = 2
HEADS_PER_GROUP = 8
D_STATE = 128
D_CONV = 5
CONV_PAD = 2
CONV_DIM = D_INNER + 2 * SSM_GROUPS * D_STATE
SSM_CHUNK = 128
CROSS_HEADS = 4
CROSS_DIM = 512
PEER_HEADS = 8
N_KEYS = 128
N_EXPERTS = N_KEYS * N_KEYS
PEER_TOPK = 16
D_KEY = 128
D_KEY_HALF = 64
EPS = 1e-6

Q_COLS = 1024
V_COLS = 1024
DT_COLS = 32
LANES = 128
SUBLANES = 8
HALO = SUBLANES
ATTN_TILE = 512
VMEM_LIMIT = 56 * 1024 * 1024


def _cparams(sem):
    return pltpu.CompilerParams(dimension_semantics=sem, vmem_limit_bytes=VMEM_LIMIT)


def _pick(n, pref):
    t = min(pref, n)
    while n % t:
        t //= 2
    return t


def _rmsnorm_kernel(x_ref, g_ref, o_ref):
    x = x_ref[...]
    y = x * lax.rsqrt(jnp.mean(x * x, axis=-1, keepdims=True) + EPS)
    o_ref[...] = (y * g_ref[...]).astype(o_ref.dtype)


def _rmsnorm_bf16(x, g):
    m, d = x.shape
    tm = _pick(m, 512)
    return pl.pallas_call(
        _rmsnorm_kernel,
        grid=(m // tm,),
        in_specs=[pl.BlockSpec((tm, d), lambda i: (i, 0)), pl.BlockSpec((1, d), lambda i: (0, 0))],
        out_specs=pl.BlockSpec((tm, d), lambda i: (i, 0)),
        out_shape=jax.ShapeDtypeStruct((m, d), BF16),
        compiler_params=_cparams(("parallel",)),
        name="rmsnorm_bf16",
    )(x, g.reshape(1, d).astype(F32))


def _mm_kernel(x_ref, w_ref, o_ref):
    o_ref[...] = jnp.dot(x_ref[...], w_ref[...], preferred_element_type=F32).astype(o_ref.dtype)


def _mm_residual_kernel(x_ref, w_ref, r_ref, o_ref):
    acc = jnp.dot(x_ref[...], w_ref[...], preferred_element_type=F32)
    o_ref[...] = (r_ref[...] + acc).astype(o_ref.dtype)


def _mm_merge_kernel(x_ref, w_ref, t_ref, g0_ref, g1_ref, o_ref):
    acc = jnp.dot(x_ref[...], w_ref[...], preferred_element_type=F32)
    g0 = jax.nn.sigmoid(g0_ref[...])
    g1 = jax.nn.sigmoid(g1_ref[...])
    o_ref[...] = (g0 * t_ref[...] + g1 * acc).astype(o_ref.dtype)


def _mm(x, w, out_dtype, *, residual=None, merge=None, tm=512, tn=1024, name="mm"):
    m, k = x.shape
    n = w.shape[1]
    tm = _pick(m, tm)
    tn = _pick(n, tn)
    in_specs = [pl.BlockSpec((tm, k), lambda i, j: (i, 0)), pl.BlockSpec((k, tn), lambda i, j: (0, j))]
    args = [x, w]
    kern = _mm_kernel
    if residual is not None:
        kern = _mm_residual_kernel
        in_specs.append(pl.BlockSpec((tm, tn), lambda i, j: (i, j)))
        args.append(residual)
    if merge is not None:
        kern = _mm_merge_kernel
        t_att, gates = merge
        nb = n // tn
        in_specs += [
            pl.BlockSpec((tm, tn), lambda i, j: (i, j)),
            pl.BlockSpec((tm, tn), lambda i, j: (i, j)),
            pl.BlockSpec((tm, tn), lambda i, j: (i, j + nb)),
        ]
        args += [t_att, gates, gates]
    return pl.pallas_call(
        kern,
        grid=(m // tm, n // tn),
        in_specs=in_specs,
        out_specs=pl.BlockSpec((tm, tn), lambda i, j: (i, j)),
        out_shape=jax.ShapeDtypeStruct((m, n), out_dtype),
        compiler_params=_cparams(("parallel", "parallel")),
        name=name,
    )(*args)


def _t5_bucket(rel):
    half = N_BUCKETS // 2
    exact = half // 2
    n = jnp.abs(rel)
    far = exact + (
        jnp.log(jnp.maximum(n, 1).astype(F32) / exact) / math.log(MAX_DISTANCE / exact) * (half - exact)
    ).astype(jnp.int32)
    far = jnp.minimum(far, half - 1)
    return jnp.where(rel > 0, half, 0) + jnp.where(n < exact, n, far)


def _attn_kernel(lam_ref, far_ref, q_ref, k_ref, v_ref, bias_ref, g_ref, o_ref, q2_scr, m_scr, l_scr, acc_scr, *, t,
                 nk, out_scale):
    h = pl.program_id(1)
    qi = pl.program_id(2)
    q = q_ref[...].astype(F32) * (ATT_QK_DIM ** -0.5)
    lane = lax.broadcasted_iota(jnp.int32, q.shape, 1)
    q2_scr[0:t, :] = jnp.where(lane < ATT_QK_DIM, q, 0.0).astype(BF16)
    q2_scr[t:2 * t, :] = jnp.where(lane >= ATT_QK_DIM, q, 0.0).astype(BF16)
    m_scr[...] = jnp.full(m_scr.shape, -jnp.inf, F32)
    l_scr[...] = jnp.zeros(l_scr.shape, F32)
    acc_scr[...] = jnp.zeros(acc_scr.shape, F32)
    nl = t // LANES

    def chunk(kc, const, tile_idx):
        start = pl.multiple_of(kc * t, t)
        kblk = k_ref[pl.ds(start, t), :]
        vblk = v_ref[pl.ds(start, t), :]
        s = lax.dot_general(q2_scr[...], kblk, (((1,), (1,)), ((), ())), preferred_element_type=F32)
        tiles = [s[:, j * LANES:(j + 1) * LANES] for j in range(nl)]
        if tile_idx is not None:
            tiles = [tl + jnp.concatenate([bias_ref[tile_idx, :, j * LANES:(j + 1) * LANES]] * 2, axis=0)
                     for j, tl in enumerate(tiles)]
        mx = functools.reduce(jnp.maximum, tiles)
        m_cur = jnp.broadcast_to(jnp.max(mx, axis=-1, keepdims=True), (2 * t, LANES))
        if const is not None:
            m_cur = m_cur + const
        m_old = m_scr[...]
        m_new = jnp.maximum(m_old, m_cur)
        alpha = jnp.exp(m_old - m_new)
        shift = -m_new if const is None else const - m_new
        ps = [jnp.exp(tl + shift) for tl in tiles]
        l_scr[...] = alpha * l_scr[...] + functools.reduce(jnp.add, ps)
        p = jnp.concatenate(ps, axis=-1).astype(BF16)
        acc_scr[...] = alpha * acc_scr[...] + jnp.dot(p, vblk, preferred_element_type=F32)
        m_scr[...] = m_new

    lo_end = jnp.maximum(qi - 1, 0)
    near_end = jnp.minimum(qi + 2, nk)
    far_lo = far_ref[h, 0]
    far_hi = far_ref[h, 1]

    def body_lo(kc, carry):
        chunk(kc, far_lo, None)
        return carry

    def body_near(kc, carry):
        chunk(kc, None, kc - qi + 1)
        return carry

    def body_hi(kc, carry):
        chunk(kc, far_hi, None)
        return carry

    lax.fori_loop(0, lo_end, body_lo, 0)
    lax.fori_loop(lo_end, near_end, body_near, 0)
    lax.fori_loop(near_end, nk, body_hi, 0)

    lam = lam_ref[0]
    l0 = jnp.sum(l_scr[0:t, :], axis=-1, keepdims=True)
    l1 = jnp.sum(l_scr[t:2 * t, :], axis=-1, keepdims=True)
    o = acc_scr[0:t, :] / l0 - lam * (acc_scr[t:2 * t, :] / l1)
    y = o * lax.rsqrt(jnp.mean(o * o, axis=-1, keepdims=True) + EPS)
    o_ref[...] = ((y * g_ref[...]) * out_scale).astype(o_ref.dtype)


def _diff_attention(qkv, b, s, rel_bias, lam, subln, out_scale, t):
    nk = s // t
    hq = ATT_HEADS
    r = jnp.arange(t, dtype=jnp.int32)
    rel = (jnp.arange(3, dtype=jnp.int32)[:, None, None] - 1) * t + r[None, None, :] - r[None, :, None]
    bucket = _t5_bucket(rel)[None]
    rb32 = rel_bias.astype(F32)
    bias_tiles = jnp.zeros((hq, 3, t, t), F32)
    for n in range(N_BUCKETS):
        bias_tiles = jnp.where(bucket == n, rb32[n][:, None, None, None], bias_tiles)
    half = N_BUCKETS // 2
    far = jnp.stack([rel_bias[half - 1], rel_bias[N_BUCKETS - 1]], axis=-1).astype(F32)
    kern = functools.partial(_attn_kernel, t=t, nk=nk, out_scale=out_scale)
    return pl.pallas_call(
        kern,
        grid=(b, hq, nk),
        in_specs=[
            pl.BlockSpec(memory_space=pltpu.SMEM),
            pl.BlockSpec(memory_space=pltpu.SMEM),
            pl.BlockSpec((t, LANES), lambda bi, h, qi: (bi * nk + qi, h)),
            pl.BlockSpec((s, LANES), lambda bi, h, qi: (bi, hq + h)),
            pl.BlockSpec((s, LANES), lambda bi, h, qi: (bi, 2 * hq + h)),
            pl.BlockSpec((None, 3, t, t), lambda bi, h, qi: (h, 0, 0, 0)),
            pl.BlockSpec((1, LANES), lambda bi, h, qi: (0, 0)),
        ],
        out_specs=pl.BlockSpec((t, LANES), lambda bi, h, qi: (bi * nk + qi, h)),
        out_shape=jax.ShapeDtypeStruct((b * s, V_COLS), BF16),
        scratch_shapes=[
            pltpu.VMEM((2 * t, LANES), BF16),
            pltpu.VMEM((2 * t, LANES), F32),
            pltpu.VMEM((2 * t, LANES), F32),
            pltpu.VMEM((2 * t, LANES), F32),
        ],
        compiler_params=_cparams(("parallel", "parallel", "parallel")),
        name="diff_attention",
    )(lam.reshape(1).astype(F32), far, qkv, qkv, qkv, bias_tiles, subln.reshape(1, LANES).astype(F32))


def _conv_kernel(prev_ref, cur_ref, next_ref, w_ref, b_ref, o_ref, ext_scr, *, tb, ns):
    si = pl.program_id(1)
    ext_scr[0:HALO, :] = jnp.where(si > 0, prev_ref[...], 0.0)
    ext_scr[HALO:HALO + tb, :] = cur_ref[...]
    ext_scr[HALO + tb:2 * HALO + tb, :] = jnp.where(si < ns - 1, next_ref[...], 0.0)
    y = b_ref[...]
    for j in range(D_CONV):
        off = HALO - CONV_PAD + j
        y = y + w_ref[j:j + 1, :] * ext_scr[off:off + tb, :]
    o_ref[...] = y * jax.nn.sigmoid(y)


def _conv_silu(xbc, b, s, conv_w, conv_b, tb=512, tc=512):
    tb = _pick(s, tb)
    ns = s // tb
    hb = tb // HALO
    nc = CONV_DIM // tc
    kern = functools.partial(_conv_kernel, tb=tb, ns=ns)
    return pl.pallas_call(
        kern,
        grid=(b, ns, nc),
        in_specs=[
            pl.BlockSpec((HALO, tc), lambda bi, si, ci: (jnp.maximum((bi * ns + si) * hb - 1, 0), ci)),
            pl.BlockSpec((tb, tc), lambda bi, si, ci: (bi * ns + si, ci)),
            pl.BlockSpec((HALO, tc), lambda bi, si, ci: (jnp.minimum((bi * ns + si + 1) * hb, b * ns * hb - 1), ci)),
            pl.BlockSpec((D_CONV, tc), lambda bi, si, ci: (0, ci)),
            pl.BlockSpec((1, tc), lambda bi, si, ci: (0, ci)),
        ],
        out_specs=pl.BlockSpec((tb, tc), lambda bi, si, ci: (bi * ns + si, ci)),
        out_shape=jax.ShapeDtypeStruct(xbc.shape, F32),
        scratch_shapes=[pltpu.VMEM((tb + 2 * HALO, tc), F32)],
        compiler_params=_cparams(("parallel", "parallel", "parallel")),
        name="conv_silu",
    )(xbc, xbc, xbc, conv_w.astype(F32), conv_b.reshape(1, CONV_DIM).astype(F32))


def _softplus(x):
    return jnp.maximum(x, 0.0) + jnp.log1p(jnp.exp(-jnp.abs(x)))


def _ssd_kernel(*refs, reverse):
    if reverse:
        x_ref, dt_ref, dtb_ref, a_ref, yf_ref, z_ref, dsk_ref, gn_ref, o_ref, st_scr = refs
    else:
        x_ref, dt_ref, dtb_ref, a_ref, o_ref, st_scr = refs
    L = SSM_CHUNK
    d = 1 if reverse else 0

    @pl.when(pl.program_id(1) == 0)
    def _():
        st_scr[...] = jnp.zeros(st_scr.shape, F32)

    dt = _softplus(dt_ref[...] + dtb_ref[...])
    a = dt * a_ref[...]
    row = lax.broadcasted_iota(jnp.int32, (L, L), 0)
    col = lax.broadcasted_iota(jnp.int32, (L, L), 1)
    tril = (row >= col).astype(F32)
    acs = jnp.dot(tril, a, preferred_element_type=F32, precision=lax.Precision.HIGHEST)
    ecs = acs - a if reverse else acs
    ecs_t = ecs.T
    keep = (col >= row) if reverse else (row >= col)
    total = acs[L - 1:L, :]

    ys = []
    for g in range(SSM_GROUPS):
        bg = x_ref[:, D_INNER + g * D_STATE:D_INNER + (g + 1) * D_STATE]
        cg = x_ref[:, D_INNER + SSM_GROUPS * D_STATE + g * D_STATE:D_INNER + SSM_GROUPS * D_STATE + (g + 1) * D_STATE]
        cg16 = cg.astype(BF16)
        cb = lax.dot_general(cg16, bg.astype(BF16), (((1,), (1,)), ((), ())), preferred_element_type=F32)
        for r in range(HEADS_PER_GROUP):
            hd = g * HEADS_PER_GROUP + r
            c = d * SSM_HEADS + hd
            e_col = ecs[:, c:c + 1]
            e_row = ecs_t[c:c + 1, :]
            tot = total[:, c:c + 1]
            if reverse:
                diff = e_row - e_col
                out_decay = jnp.exp(tot - e_col)
                in_decay = jnp.exp(e_col)
            else:
                diff = e_col - e_row
                out_decay = jnp.exp(e_col)
                in_decay = jnp.exp(tot - e_col)
            seg = jnp.exp(jnp.where(keep, diff, -jnp.inf))
            xdt = (x_ref[:, hd * SSM_HEADDIM:(hd + 1) * SSM_HEADDIM] * dt[:, c:c + 1]).astype(BF16)
            y = jnp.dot((seg * cb).astype(BF16), xdt, preferred_element_type=F32)
            st = st_scr[hd]
            y = y + jnp.dot(cg16, st.astype(BF16), preferred_element_type=F32) * out_decay
            upd = lax.dot_general((bg * in_decay).astype(BF16), xdt, (((0,), (0,)), ((), ())),
                                  preferred_element_type=F32)
            st_scr[hd] = st * jnp.exp(tot) + upd
            ys.append(y)
    y = jnp.concatenate(ys, axis=-1)
    if not reverse:
        o_ref[...] = y
        return
    xs = x_ref[:, 0:D_INNER]
    y = y + yf_ref[...] + dsk_ref[...] * xs
    z = z_ref[...]
    y = y * (z * jax.nn.sigmoid(z))
    gw = D_INNER // SSM_GROUPS
    outs = []
    for g in range(SSM_GROUPS):
        yg = y[:, g * gw:(g + 1) * gw]
        yn = yg * lax.rsqrt(jnp.mean(yg * yg, axis=-1, keepdims=True) + EPS)
        outs.append(yn * gn_ref[:, g * gw:(g + 1) * gw])
    o_ref[...] = jnp.concatenate(outs, axis=-1).astype(o_ref.dtype)


def _ssd(xbc_act, dt_raw, z, b, s, dt_bias, a_log, d_skip, ssm_norm):
    L = SSM_CHUNK
    nc = s // L
    pad = LANES - DT_COLS
    dtb = jnp.pad(dt_bias.astype(F32).reshape(1, DT_COLS), ((0, 0), (0, pad)))
    a_neg = jnp.pad(-jnp.exp(a_log.astype(F32)).reshape(1, DT_COLS), ((0, 0), (0, pad)))
    dsk = d_skip.astype(F32)
    dsk = jnp.repeat(dsk[0] + dsk[1], SSM_HEADDIM).reshape(1, D_INNER)
    st_shape = pltpu.VMEM((SSM_HEADS, D_STATE, SSM_HEADDIM), F32)
    small = lambda w: pl.BlockSpec((1, w), lambda bi, ci: (0, 0))

    fwd_map = lambda bi, ci: (bi * nc + ci, 0)
    y_f = pl.pallas_call(
        functools.partial(_ssd_kernel, reverse=False),
        grid=(b, nc),
        in_specs=[pl.BlockSpec((L, CONV_DIM), fwd_map), pl.BlockSpec((L, LANES), fwd_map), small(LANES), small(LANES)],
        out_specs=pl.BlockSpec((L, D_INNER), fwd_map),
        out_shape=jax.ShapeDtypeStruct((b * s, D_INNER), F32),
        scratch_shapes=[st_shape],
        compiler_params=_cparams(("parallel", "arbitrary")),
        name="ssd_fwd",
    )(xbc_act, dt_raw, dtb, a_neg)

    bwd_map = lambda bi, ci: (bi * nc + (nc - 1 - ci), 0)
    return pl.pallas_call(
        functools.partial(_ssd_kernel, reverse=True),
        grid=(b, nc),
        in_specs=[
            pl.BlockSpec((L, CONV_DIM), bwd_map),
            pl.BlockSpec((L, LANES), bwd_map),
            small(LANES),
            small(LANES),
            pl.BlockSpec((L, D_INNER), bwd_map),
            pl.BlockSpec((L, D_INNER), bwd_map),
            small(D_INNER),
            small(D_INNER),
        ],
        out_specs=pl.BlockSpec((L, D_INNER), bwd_map),
        out_shape=jax.ShapeDtypeStruct((b * s, D_INNER), BF16),
        scratch_shapes=[st_shape],
        compiler_params=_cparams(("parallel", "arbitrary")),
        name="ssd_bwd",
    )(xbc_act, dt_raw, dtb, a_neg, y_f, z, dsk, ssm_norm.astype(F32).reshape(1, D_INNER))


def _cross_kernel(q_ref, k_ref, v_ref, o_ref):
    outs = []
    for h in range(CROSS_HEADS):
        sl = slice(h * CROSS_DIM, (h + 1) * CROSS_DIM)
        s = lax.dot_general(q_ref[:, sl], k_ref[:, sl], (((1,), (1,)), ((), ())), preferred_element_type=F32)
        s = s * (CROSS_DIM ** -0.5)
        p = jnp.exp(s - jnp.max(s, axis=-1, keepdims=True))
        p = p / jnp.sum(p, axis=-1, keepdims=True)
        outs.append(jnp.dot(p.astype(BF16), v_ref[:, sl], preferred_element_type=F32))
    o_ref[...] = jnp.concatenate(outs, axis=-1).astype(o_ref.dtype)


def _cross_core(q, kv, b, s, n_mem, tq=512):
    tq = _pick(s, tq)
    nq = s // tq
    return pl.pallas_call(
        _cross_kernel,
        grid=(b, nq),
        in_specs=[
            pl.BlockSpec((tq, D_MODEL), lambda bi, qi: (bi * nq + qi, 0)),
            pl.BlockSpec((n_mem, D_MODEL), lambda bi, qi: (bi, 0)),
            pl.BlockSpec((n_mem, D_MODEL), lambda bi, qi: (bi, 1)),
        ],
        out_specs=pl.BlockSpec((tq, D_MODEL), lambda bi, qi: (bi * nq + qi, 0)),
        out_shape=jax.ShapeDtypeStruct((b * s, D_MODEL), BF16),
        compiler_params=_cparams(("parallel", "parallel")),
        name="cross_attention",
    )(q, kv, kv)


def _top16(s, vals_scr, idx_scr, payload=None):
    n = s.shape[0]
    pos = lax.broadcasted_iota(jnp.int32, s.shape, 0).astype(F32)
    for k in range(PEER_TOPK):
        m = jnp.max(s, axis=0, keepdims=True)
        first = jnp.min(jnp.where(s == m, pos, float(n)), axis=0, keepdims=True)
        sel = pos == first
        vals_scr[k:k + 1, :] = m
        if payload is None:
            idx_scr[k:k + 1, :] = first
        else:
            idx_scr[k:k + 1, :] = jnp.max(jnp.where(sel, payload, -1.0), axis=0, keepdims=True)
        s = jnp.where(sel, -jnp.inf, s)


def _topk_kernel(q_ref, keys_ref, idx_ref, gate_ref, v1_scr, i1_scr, v2_scr, i2_scr, tv_scr, ti_scr):
    for h in range(PEER_HEADS):
        qh = q_ref[:, h * D_KEY:(h + 1) * D_KEY]
        sc = lax.dot_general(keys_ref[h], qh, (((1,), (1,)), ((), ())), preferred_element_type=F32)
        _top16(sc[0:N_KEYS], v1_scr, i1_scr)
        _top16(sc[N_KEYS:2 * N_KEYS], v2_scr, i2_scr)
        hs = SUBLANES
        pairs = [(slice(0, 1), slice(0, hs)), (slice(0, 1), slice(hs, 2 * hs))]
        pairs += [(slice(i, i + 1), slice(0, hs)) for i in range(1, hs)]
        pairs += [(slice(hs, 2 * hs), slice(0, 1))]
        cand = jnp.concatenate([v1_scr[a, :] + v2_scr[b, :] for a, b in pairs], axis=0)
        eidx = jnp.concatenate([i1_scr[a, :] * N_KEYS + i2_scr[b, :] for a, b in pairs], axis=0)
        _top16(cand, tv_scr, ti_scr, payload=eidx)
        top = tv_scr[...]
        e = jnp.exp(top - top[0:1, :])
        gate_ref[h * PEER_TOPK:(h + 1) * PEER_TOPK, :] = e / jnp.sum(e, axis=0, keepdims=True)
        idx_ref[h * PEER_TOPK:(h + 1) * PEER_TOPK, :] = ti_scr[...]


def _peer_topk(pq, keys_bd, tb=128):
    tt = pq.shape[0]
    nhk = PEER_HEADS * PEER_TOPK
    sc16 = lambda: pltpu.VMEM((PEER_TOPK, tb), F32)
    return pl.pallas_call(
        _topk_kernel,
        grid=(tt // tb,),
        in_specs=[
            pl.BlockSpec((tb, PEER_HEADS * D_KEY), lambda i: (i, 0)),
            pl.BlockSpec((PEER_HEADS, 2 * N_KEYS, D_KEY), lambda i: (0, 0, 0)),
        ],
        out_specs=[pl.BlockSpec((nhk, tb), lambda i: (0, i)), pl.BlockSpec((nhk, tb), lambda i: (0, i))],
        out_shape=[jax.ShapeDtypeStruct((nhk, tt), F32), jax.ShapeDtypeStruct((nhk, tt), F32)],
        scratch_shapes=[sc16() for _ in range(6)],
        compiler_params=_cparams(("parallel",)),
        name="peer_topk",
    )(pq, keys_bd)


GATE_GROUP = 16
GATE_PITCH = N_KEYS + 8


def _gate_kernel(idx_ref, gate_ref, o_ref, idx_scr, gate_scr, g_scr, *, tg):
    idx_scr[...] = idx_ref[...].T
    gate_scr[...] = gate_ref[...].T
    pos = lax.broadcasted_iota(jnp.int32, (N_KEYS, PEER_HEADS * PEER_TOPK), 0)
    half = GATE_GROUP // 2

    def body(gi, carry):
        base = pl.multiple_of(gi * GATE_GROUP, GATE_GROUP)
        for u in range(GATE_GROUP):
            e = idx_scr[pl.ds(base + u, 1), :].astype(jnp.int32)
            g = gate_scr[pl.ds(base + u, 1), :]
            a_t = jnp.where(pos == (e >> 7), 1.0, 0.0).astype(BF16)
            b_t = jnp.where(pos == (e & (N_KEYS - 1)), g, 0.0).astype(BF16)
            g_scr[u * GATE_PITCH:u * GATE_PITCH + N_KEYS, :] = lax.dot_general(
                a_t, b_t, (((1,), (1,)), ((), ())), preferred_element_type=F32)
        for e1 in range(N_KEYS):
            lo = g_scr[pl.ds(e1, half, stride=GATE_PITCH), :]
            hi = g_scr[pl.ds(half * GATE_PITCH + e1, half, stride=GATE_PITCH), :]
            o_ref[e1, pl.ds(base, GATE_GROUP), :] = jnp.concatenate([lo, hi], axis=0).astype(o_ref.dtype)
        return carry

    lax.fori_loop(0, tg // GATE_GROUP, body, 0)


def _peer_gates(idx, gate, tg=128):
    nhk, tt = idx.shape
    return pl.pallas_call(
        functools.partial(_gate_kernel, tg=tg),
        grid=(tt // tg,),
        in_specs=[pl.BlockSpec((nhk, tg), lambda i: (0, i)), pl.BlockSpec((nhk, tg), lambda i: (0, i))],
        out_specs=pl.BlockSpec((N_KEYS, tg, N_KEYS), lambda i: (0, i, 0)),
        out_shape=jax.ShapeDtypeStruct((N_KEYS, tt, N_KEYS), BF16),
        scratch_shapes=[
            pltpu.VMEM((tg, nhk), F32),
            pltpu.VMEM((tg, nhk), F32),
            pltpu.VMEM((GATE_GROUP * GATE_PITCH, N_KEYS), F32),
        ],
        compiler_params=_cparams(("parallel",)),
        name="peer_gates",
    )(idx, gate)


def _gelu_tanh(x):
    return x * (0.5 * (1.0 + jnp.tanh(math.sqrt(2.0 / math.pi) * (x + 0.044715 * (x * x * x)))))


def _ffn_kernel(h_ref, u_ref, v_ref, g_ref, x_ref, nf_ref, o_ref):
    j = pl.program_id(1)

    @pl.when(j == 0)
    def _():
        o_ref[...] = jnp.zeros(o_ref.shape, F32)

    s = lax.dot_general(h_ref[...], u_ref[...], (((1,), (1,)), ((), ())), preferred_element_type=F32)
    coeff = jnp.concatenate(
        [g_ref[k].astype(F32) * _gelu_tanh(s[:, k * LANES:(k + 1) * LANES]) for k in range(g_ref.shape[0])], axis=-1
    ).astype(BF16)
    o_ref[...] += jnp.dot(coeff, v_ref[...], preferred_element_type=F32)

    @pl.when(j == pl.num_programs(1) - 1)
    def _():
        x = x_ref[...] + o_ref[...]
        y = x * lax.rsqrt(jnp.mean(x * x, axis=-1, keepdims=True) + EPS)
        o_ref[...] = y * nf_ref[...]


def _peer_ffn(hp, u16, v16, gmat, x_res, norm_final, tm=512, te=1024):
    tt = hp.shape[0]
    tm = _pick(tt, tm)
    return pl.pallas_call(
        _ffn_kernel,
        grid=(tt // tm, N_EXPERTS // te),
        in_specs=[
            pl.BlockSpec((tm, D_MODEL), lambda i, j: (i, 0)),
            pl.BlockSpec((te, D_MODEL), lambda i, j: (j, 0)),
            pl.BlockSpec((te, D_MODEL), lambda i, j: (j, 0)),
            pl.BlockSpec((te // LANES, tm, LANES), lambda i, j: (j, i, 0)),
            pl.BlockSpec((tm, D_MODEL), lambda i, j: (i, 0)),
            pl.BlockSpec((1, D_MODEL), lambda i, j: (0, 0)),
        ],
        out_specs=pl.BlockSpec((tm, D_MODEL), lambda i, j: (i, 0)),
        out_shape=jax.ShapeDtypeStruct((tt, D_MODEL), F32),
        compiler_params=_cparams(("parallel", "arbitrary")),
        name="peer_ffn",
    )(hp, u16, v16, gmat, x_res, norm_final.reshape(1, D_MODEL).astype(F32))


def _prepare(p):
    w_in = p["w_in"][0]
    c = np.cumsum([0, 3 * Q_COLS, D_INNER, CONV_DIM, DT_COLS, 2 * D_MODEL])
    w = {}
    w["qkv"] = w_in[:, c[0]:c[1]].astype(BF16)
    w["z"] = w_in[:, c[1]:c[2]].astype(BF16)
    w["xbc"] = w_in[:, c[2]:c[3]].astype(BF16)
    w["dt"] = jnp.pad(w_in[:, c[3]:c[4]], ((0, 0), (0, LANES - DT_COLS))).astype(BF16)
    w["gates"] = w_in[:, c[4]:c[5]].astype(BF16)
    for name in ("w_attn_o", "w_ssm_o", "w_out", "w_cq", "w_ckv", "w_co", "w_pq", "expert_u", "expert_v"):
        w[name] = p[name][0].astype(BF16)
    sk = p["sub_keys"][0].astype(BF16)
    zero = jnp.zeros_like(sk[:, 0])
    w["keys_bd"] = jnp.concatenate(
        [jnp.concatenate([sk[:, 0], zero], axis=-1), jnp.concatenate([zero, sk[:, 1]], axis=-1)], axis=1
    )
    lam_init = 0.8 - 0.6 * math.exp(-0.3 * 0)
    w["lam"] = (
        jnp.exp(jnp.sum(p["lam_q1"][0].astype(F32) * p["lam_k1"][0].astype(F32)))
        - jnp.exp(jnp.sum(p["lam_q2"][0].astype(F32) * p["lam_k2"][0].astype(F32)))
        + lam_init
    )
    w["lam_init"] = lam_init
    return w


def _trunk(x, mem, p, w):
    b, s, d = x.shape
    n_mem = mem.shape[1]
    xf = x.reshape(b * s, d)

    h1 = _rmsnorm_bf16(xf, p["norm_mix"][0])
    qkv = _mm(h1, w["qkv"], BF16, name="mm_qkv")
    z = _mm(h1, w["z"], F32, name="mm_z")
    xbc = _mm(h1, w["xbc"], F32, tn=512, name="mm_xbc")
    dt_raw = _mm(h1, w["dt"], F32, name="mm_dt")
    gates = _mm(h1, w["gates"], F32, name="mm_gates")

    o_att = _diff_attention(qkv, b, s, p["rel_bias"], w["lam"], p["attn_subln"][0], 1.0 - w["lam_init"],
                            _pick(s, ATTN_TILE))
    xbc_act = _conv_silu(xbc, b, s, p["conv_w"][0], p["conv_b"][0])
    y_ssm = _ssd(xbc_act, dt_raw, z, b, s, p["dt_bias"][0], p["a_log"][0], p["d_skip"][0], p["ssm_norm"][0])

    t_att = _mm(o_att, w["w_attn_o"], F32, name="mm_attn_o")
    merged = _mm(y_ssm, w["w_ssm_o"], BF16, merge=(t_att, gates), name="mm_ssm_o_merge")
    x1 = _mm(merged, w["w_out"], F32, residual=xf, name="mm_out")

    hq = _rmsnorm_bf16(x1, p["norm_cross"][0])
    q = _mm(hq, w["w_cq"], BF16, name="mm_cq")
    mn = _rmsnorm_bf16(mem.reshape(b * n_mem, d), p["norm_mem"][0])
    kv = _mm(mn, w["w_ckv"], BF16, name="mm_ckv")
    oc = _cross_core(q, kv, b, s, n_mem)
    x2 = _mm(oc, w["w_co"], F32, residual=x1, name="mm_co")

    hp = _rmsnorm_bf16(x2, p["norm_ffn"][0])
    pq = _mm(hp, w["w_pq"], BF16, name="mm_pq")
    idx, gate = _peer_topk(pq, w["keys_bd"])
    gmat = _peer_gates(idx, gate)
    y = _peer_ffn(hp, w["expert_u"], w["expert_v"], gmat, x2, p["norm_final"])
    return y.reshape(b, s, d)


def kernel(x_prompt, x_sample, mem_prompt, mem_sample, norm_mix, w_in, lam_q1, lam_k1, lam_q2, lam_k2, rel_bias, attn_subln, w_attn_o, conv_w, conv_b, a_log, dt_bias, d_skip, ssm_norm, w_ssm_o, w_out, norm_cross, norm_mem, w_cq, w_ckv, w_co, norm_ffn, w_pq, sub_keys, expert_u, expert_v, norm_final):
    p = dict(norm_mix=norm_mix, w_in=w_in, lam_q1=lam_q1, lam_k1=lam_k1, lam_q2=lam_q2, lam_k2=lam_k2,
             rel_bias=rel_bias, attn_subln=attn_subln, w_attn_o=w_attn_o, conv_w=conv_w, conv_b=conv_b,
             a_log=a_log, dt_bias=dt_bias, d_skip=d_skip, ssm_norm=ssm_norm, w_ssm_o=w_ssm_o, w_out=w_out,
             norm_cross=norm_cross, norm_mem=norm_mem, w_cq=w_cq, w_ckv=w_ckv, w_co=w_co, norm_ffn=norm_ffn,
             w_pq=w_pq, sub_keys=sub_keys, expert_u=expert_u, expert_v=expert_v, norm_final=norm_final)
    w = _prepare(p)
    y_prompt = _trunk(x_prompt, mem_prompt, p, w)
    y_sample = _trunk(x_sample, mem_sample, p, w)
    return (y_prompt, y_sample)
```

```python
import functools
import math

import jax
import jax.numpy as jnp
import numpy as np
from jax import lax
from jax.experimental import pallas as pl
from jax.experimental.pallas import tpu as pltpu

F32 = jnp.float32
BF16 = jnp.bfloat16

D_MODEL = 2048
ATT_HEADS = 8
ATT_V_DIM = 128
ATT_QK_DIM = 64
N_BUCKETS = 32
MAX_DISTANCE = 128
D_INNER = 1024
SSM_HEADDIM = 64
SSM_HEADS = 16
SSM_GROUPS = 2
HEADS_PER_GROUP = 8
D_STATE = 128
D_CONV = 5
CONV_PAD = 2
CONV_DIM = D_INNER + 2 * SSM_GROUPS * D_STATE
SSM_CHUNK = 128
CROSS_HEADS = 4
CROSS_DIM = 512
PEER_HEADS = 8
N_KEYS = 128
N_EXPERTS = N_KEYS * N_KEYS
PEER_TOPK = 16
D_KEY = 128
D_KEY_HALF = 64
EPS = 1e-6

Q_COLS = 1024
V_COLS = 1024
DT_COLS = 32
LANES = 128
SUBLANES = 8
HALO = SUBLANES
ATTN_TILE = 512
ONES_ROWS = 16
ATTN_GROUP = 256
VMEM_LIMIT = 56 * 1024 * 1024


def _cparams(sem):
    return pltpu.CompilerParams(dimension_semantics=sem, vmem_limit_bytes=VMEM_LIMIT)


def _pick(n, pref):
    t = min(pref, n)
    while n % t:
        t //= 2
    return t


def _rmsnorm_kernel(x_ref, g_ref, o_ref):
    x = x_ref[...]
    y = x * lax.rsqrt(jnp.mean(x * x, axis=-1, keepdims=True) + EPS)
    o_ref[...] = (y * g_ref[...]).astype(o_ref.dtype)


def _rmsnorm_bf16(x, g):
    m, d = x.shape
    tm = _pick(m, 512)
    return pl.pallas_call(
        _rmsnorm_kernel,
        grid=(m // tm,),
        in_specs=[pl.BlockSpec((tm, d), lambda i: (i, 0)), pl.BlockSpec((1, d), lambda i: (0, 0))],
        out_specs=pl.BlockSpec((tm, d), lambda i: (i, 0)),
        out_shape=jax.ShapeDtypeStruct((m, d), BF16),
        compiler_params=_cparams(("parallel",)),
        name="rmsnorm_bf16",
    )(x, g.reshape(1, d).astype(F32))


def _mm_kernel(x_ref, w_ref, o_ref):
    o_ref[...] = jnp.dot(x_ref[...], w_ref[...], preferred_element_type=F32).astype(o_ref.dtype)


def _mm_residual_kernel(x_ref, w_ref, r_ref, o_ref):
    acc = jnp.dot(x_ref[...], w_ref[...], preferred_element_type=F32)
    o_ref[...] = (r_ref[...] + acc).astype(o_ref.dtype)


def _mm_merge_kernel(x_ref, w_ref, t_ref, g0_ref, g1_ref, o_ref):
    acc = jnp.dot(x_ref[...], w_ref[...], preferred_element_type=F32)
    g0 = jax.nn.sigmoid(g0_ref[...])
    g1 = jax.nn.sigmoid(g1_ref[...])
    o_ref[...] = (g0 * t_ref[...] + g1 * acc).astype(o_ref.dtype)


def _mm(x, w, out_dtype, *, residual=None, merge=None, tm=1024, tn=512, name="mm"):
    m, k = x.shape
    n = w.shape[1]
    tm = _pick(m, tm)
    tn = _pick(n, tn)
    in_specs = [pl.BlockSpec((tm, k), lambda i, j: (i, 0)), pl.BlockSpec((k, tn), lambda i, j: (0, j))]
    args = [x, w]
    kern = _mm_kernel
    if residual is not None:
        kern = _mm_residual_kernel
        in_specs.append(pl.BlockSpec((tm, tn), lambda i, j: (i, j)))
        args.append(residual)
    if merge is not None:
        kern = _mm_merge_kernel
        t_att, gates = merge
        nb = n // tn
        in_specs += [
            pl.BlockSpec((tm, tn), lambda i, j: (i, j)),
            pl.BlockSpec((tm, tn), lambda i, j: (i, j)),
            pl.BlockSpec((tm, tn), lambda i, j: (i, j + nb)),
        ]
        args += [t_att, gates, gates]
    return pl.pallas_call(
        kern,
        grid=(m // tm, n // tn),
        in_specs=in_specs,
        out_specs=pl.BlockSpec((tm, tn), lambda i, j: (i, j)),
        out_shape=jax.ShapeDtypeStruct((m, n), out_dtype),
        compiler_params=_cparams(("parallel", "parallel")),
        name=name,
    )(*args)


def _t5_bucket(rel):
    half = N_BUCKETS // 2
    exact = half // 2
    n = jnp.abs(rel)
    far = exact + (
        jnp.log(jnp.maximum(n, 1).astype(F32) / exact) / math.log(MAX_DISTANCE / exact) * (half - exact)
    ).astype(jnp.int32)
    far = jnp.minimum(far, half - 1)
    return jnp.where(rel > 0, half, 0) + jnp.where(n < exact, n, far)


def _attn_kernel(lam_ref, far_ref, q_ref, k_ref, vt_ref, bias_ref, g_ref, o_ref, q2_scr, sa_scr, sb_scr, m_scr, acc_scr, *, t,
                 nk, out_scale):
    h = pl.program_id(1)
    qi = pl.program_id(2)
    q = q_ref[...].astype(F32) * (ATT_QK_DIM ** -0.5)
    lane = lax.broadcasted_iota(jnp.int32, q.shape, 1)
    q2_scr[0:t, :] = jnp.where(lane < ATT_QK_DIM, q, 0.0).astype(BF16)
    q2_scr[t:2 * t, :] = jnp.where(lane >= ATT_QK_DIM, q, 0.0).astype(BF16)
    m_scr[...] = jnp.full(m_scr.shape, -jnp.inf, F32)
    acc_scr[...] = jnp.zeros(acc_scr.shape, F32)

    gw = ATTN_GROUP
    ngroups = 2 * t // gw

    def scores(kc, g):
        start = pl.multiple_of(kc * t, t)
        return lax.dot_general(k_ref[pl.ds(start, t), :], q2_scr[g * gw:(g + 1) * gw, :], (((1,), (1,)), ((), ())),
                               preferred_element_type=F32)

    def update(g, st, vt, const, tile_idx):
        cols = slice(g * gw, (g + 1) * gw)
        if tile_idx is not None:
            b0 = (g * gw) % t
            st = st + bias_ref[tile_idx, :, b0:b0 + gw]
        m_cur = jnp.max(st, axis=0, keepdims=True)
        if const is not None:
            m_cur = m_cur + const
        m_old = m_scr[:, cols]
        m_new = jnp.maximum(m_old, m_cur)
        alpha = jnp.exp(m_old - m_new)
        shift = -m_new if const is None else const - m_new
        pt = jnp.exp(st + shift)
        acc_scr[:, cols] = alpha * acc_scr[:, cols] + jnp.dot(vt, pt.astype(BF16), preferred_element_type=F32)
        m_scr[:, cols] = m_new

    lo_end = jnp.maximum(qi - 1, 0)
    near_end = jnp.minimum(qi + 2, nk)
    three = (near_end - lo_end) == 3
    t_start = lo_end - jnp.where(three & (qi + 2 > nk - 1), 1, 0)
    n_tile = jnp.where(three, 4, 2)
    n_const = nk - n_tile

    def tile_chunk(j):
        kc = jnp.minimum(t_start + j, nk - 1)
        rel = kc - qi
        return kc, jnp.where(rel < -1, 3, jnp.where(rel > 1, 4, rel + 1))

    def const_chunk(f):
        f = jnp.minimum(f, n_const - 1)
        lo = f < t_start
        return jnp.where(lo, f, f + n_tile), jnp.where(lo, far_ref[h, 0], far_ref[h, 1])

    def pipeline(n_chunks, chunk_of, use_tile):
        @pl.when(n_chunks >= 2)
        def _():
            k0, _ = chunk_of(0)
            for g in range(ngroups):
                sa_scr[:, g * gw:(g + 1) * gw] = scores(k0, g)

        def pair_body(j, carry):
            (ka, xa), (kb, xb), (kn, _) = chunk_of(2 * j), chunk_of(2 * j + 1), chunk_of(2 * j + 2)
            for src, dst, k_cur, k_nxt, x in ((sa_scr, sb_scr, ka, kb, xa), (sb_scr, sa_scr, kb, kn, xb)):
                vt = vt_ref[k_cur]
                for g in range(ngroups):
                    cols = slice(g * gw, (g + 1) * gw)
                    dst[:, cols] = scores(k_nxt, g)
                    if use_tile:
                        update(g, src[:, cols], vt, None, x)
                    else:
                        update(g, src[:, cols], vt, x, None)
            return carry

        lax.fori_loop(0, n_chunks // 2, pair_body, 0)

    pipeline(n_tile, tile_chunk, True)
    pipeline(n_const, const_chunk, False)

    lam = lam_ref[0]
    dv = ATT_V_DIM
    ot = (acc_scr[0:dv, 0:t] / acc_scr[dv:dv + 1, 0:t]
          - lam * (acc_scr[0:dv, t:2 * t] / acc_scr[dv:dv + 1, t:2 * t]))
    o = ot.T
    y = o * lax.rsqrt(jnp.mean(o * o, axis=-1, keepdims=True) + EPS)
    o_ref[...] = ((y * g_ref[...]) * out_scale).astype(o_ref.dtype)


def _diff_attention(qkv, b, s, rel_bias, lam, subln, out_scale, t):
    nk = s // t
    hq = ATT_HEADS
    r = jnp.arange(t, dtype=jnp.int32)
    rel = (jnp.arange(3, dtype=jnp.int32)[:, None, None] - 1) * t + r[None, :, None] - r[None, None, :]
    bucket = _t5_bucket(rel)[None]
    rb32 = rel_bias.astype(F32)
    bias_tiles = jnp.zeros((hq, 3, t, t), F32)
    for n in range(N_BUCKETS):
        bias_tiles = jnp.where(bucket == n, rb32[n][:, None, None, None], bias_tiles)
    half = N_BUCKETS // 2
    far = jnp.stack([rel_bias[half - 1], rel_bias[N_BUCKETS - 1]], axis=-1).astype(F32)
    bias_tiles = jnp.concatenate([bias_tiles, jnp.broadcast_to(far[:, :, None, None], (hq, 2, t, t))], axis=1)
    assert nk % 2 == 0 and t % ATTN_GROUP == 0
    vt = qkv[:, 2 * Q_COLS:].reshape(b, nk, t, hq, ATT_V_DIM).transpose(0, 3, 1, 4, 2)
    vt = jnp.concatenate([vt, jnp.ones((b, hq, nk, ONES_ROWS, t), BF16)], axis=3)
    kern = functools.partial(_attn_kernel, t=t, nk=nk, out_scale=out_scale)
    return pl.pallas_call(
        kern,
        grid=(b, hq, nk),
        in_specs=[
            pl.BlockSpec(memory_space=pltpu.SMEM),
            pl.BlockSpec(memory_space=pltpu.SMEM),
            pl.BlockSpec((t, LANES), lambda bi, h, qi: (bi * nk + qi, h)),
            pl.BlockSpec((s, LANES), lambda bi, h, qi: (bi, hq + h)),
            pl.BlockSpec((None, None, nk, ATT_V_DIM + ONES_ROWS, t), lambda bi, h, qi: (bi, h, 0, 0, 0)),
            pl.BlockSpec((None, 5, t, t), lambda bi, h, qi: (h, 0, 0, 0)),
            pl.BlockSpec((1, LANES), lambda bi, h, qi: (0, 0)),
        ],
        out_specs=pl.BlockSpec((t, LANES), lambda bi, h, qi: (bi * nk + qi, h)),
        out_shape=jax.ShapeDtypeStruct((b * s, V_COLS), BF16),
        scratch_shapes=[
            pltpu.VMEM((2 * t, LANES), BF16),
            pltpu.VMEM((t, 2 * t), F32),
            pltpu.VMEM((t, 2 * t), F32),
            pltpu.VMEM((1, 2 * t), F32),
            pltpu.VMEM((ATT_V_DIM + ONES_ROWS, 2 * t), F32),
        ],
        compiler_params=_cparams(("parallel", "parallel", "parallel")),
        name="diff_attention",
    )(lam.reshape(1).astype(F32), far, qkv, qkv, vt, bias_tiles, subln.reshape(1, LANES).astype(F32))


def _conv_kernel(prev_ref, cur_ref, next_ref, w_ref, b_ref, o_ref, ext_scr, *, tb, ns):
    si = pl.program_id(1)
    ext_scr[0:HALO, :] = jnp.where(si > 0, prev_ref[...], 0.0)
    ext_scr[HALO:HALO + tb, :] = cur_ref[...]
    ext_scr[HALO + tb:2 * HALO + tb, :] = jnp.where(si < ns - 1, next_ref[...], 0.0)
    y = b_ref[...]
    for j in range(D_CONV):
        off = HALO - CONV_PAD + j
        y = y + w_ref[j:j + 1, :] * ext_scr[off:off + tb, :]
    o_ref[...] = y * jax.nn.sigmoid(y)


def _conv_silu(xbc, b, s, conv_w, conv_b, tb=512, tc=512):
    tb = _pick(s, tb)
    ns = s // tb
    hb = tb // HALO
    nc = CONV_DIM // tc
    kern = functools.partial(_conv_kernel, tb=tb, ns=ns)
    return pl.pallas_call(
        kern,
        grid=(b, ns, nc),
        in_specs=[
            pl.BlockSpec((HALO, tc), lambda bi, si, ci: (jnp.maximum((bi * ns + si) * hb - 1, 0), ci)),
            pl.BlockSpec((tb, tc), lambda bi, si, ci: (bi * ns + si, ci)),
            pl.BlockSpec((HALO, tc), lambda bi, si, ci: (jnp.minimum((bi * ns + si + 1) * hb, b * ns * hb - 1), ci)),
            pl.BlockSpec((D_CONV, tc), lambda bi, si, ci: (0, ci)),
            pl.BlockSpec((1, tc), lambda bi, si, ci: (0, ci)),
        ],
        out_specs=pl.BlockSpec((tb, tc), lambda bi, si, ci: (bi * ns + si, ci)),
        out_shape=jax.ShapeDtypeStruct(xbc.shape, F32),
        scratch_shapes=[pltpu.VMEM((tb + 2 * HALO, tc), F32)],
        compiler_params=_cparams(("parallel", "parallel", "parallel")),
        name="conv_silu",
    )(xbc, xbc, xbc, conv_w.astype(F32), conv_b.reshape(1, CONV_DIM).astype(F32))


def _softplus(x):
    return jnp.maximum(x, 0.0) + jnp.log1p(jnp.exp(-jnp.abs(x)))


def _ssd_kernel(*refs, reverse):
    if reverse:
        x_ref, dt_ref, dtb_ref, a_ref, yf_ref, z_ref, dsk_ref, gn_ref, o_ref, st_scr = refs
    else:
        x_ref, dt_ref, dtb_ref, a_ref, o_ref, st_scr = refs
    L = SSM_CHUNK
    d = 1 if reverse else 0

    @pl.when(pl.program_id(1) == 0)
    def _():
        st_scr[...] = jnp.zeros(st_scr.shape, F32)

    dt = _softplus(dt_ref[...] + dtb_ref[...])
    a = dt * a_ref[...]
    row = lax.broadcasted_iota(jnp.int32, (L, L), 0)
    col = lax.broadcasted_iota(jnp.int32, (L, L), 1)
    tril = (row >= col).astype(F32)
    acs = jnp.dot(tril, a, preferred_element_type=F32, precision=lax.Precision.HIGHEST)
    ecs = acs - a if reverse else acs
    ecs_t = ecs.T
    keep = (col >= row) if reverse else (row >= col)
    total = acs[L - 1:L, :]

    ys = []
    for g in range(SSM_GROUPS):
        bg = x_ref[:, D_INNER + g * D_STATE:D_INNER + (g + 1) * D_STATE]
        cg = x_ref[:, D_INNER + SSM_GROUPS * D_STATE + g * D_STATE:D_INNER + SSM_GROUPS * D_STATE + (g + 1) * D_STATE]
        cg16 = cg.astype(BF16)
        cb = lax.dot_general(cg16, bg.astype(BF16), (((1,), (1,)), ((), ())), preferred_element_type=F32)
        for r in range(HEADS_PER_GROUP):
            hd = g * HEADS_PER_GROUP + r
            c = d * SSM_HEADS + hd
            e_col = ecs[:, c:c + 1]
            e_row = ecs_t[c:c + 1, :]
            tot = total[:, c:c + 1]
            if reverse:
                diff = e_row - e_col
                out_decay = jnp.exp(tot - e_col)
                in_decay = jnp.exp(e_col)
            else:
                diff = e_col - e_row
                out_decay = jnp.exp(e_col)
                in_decay = jnp.exp(tot - e_col)
            seg = jnp.exp(jnp.where(keep, diff, -jnp.inf))
            xdt = (x_ref[:, hd * SSM_HEADDIM:(hd + 1) * SSM_HEADDIM] * dt[:, c:c + 1]).astype(BF16)
            y = jnp.dot((seg * cb).astype(BF16), xdt, preferred_element_type=F32)
            st = st_scr[hd]
            y = y + jnp.dot(cg16, st.astype(BF16), preferred_element_type=F32) * out_decay
            upd = lax.dot_general((bg * in_decay).astype(BF16), xdt, (((0,), (0,)), ((), ())),
                                  preferred_element_type=F32)
            st_scr[hd] = st * jnp.exp(tot) + upd
            ys.append(y)
    y = jnp.concatenate(ys, axis=-1)
    if not reverse:
        o_ref[...] = y
        return
    xs = x_ref[:, 0:D_INNER]
    y = y + yf_ref[...] + dsk_ref[...] * xs
    z = z_ref[...]
    y = y * (z * jax.nn.sigmoid(z))
    gw = D_INNER // SSM_GROUPS
    outs = []
    for g in range(SSM_GROUPS):
        yg = y[:, g * gw:(g + 1) * gw]
        yn = yg * lax.rsqrt(jnp.mean(yg * yg, axis=-1, keepdims=True) + EPS)
        outs.append(yn * gn_ref[:, g * gw:(g + 1) * gw])
    o_ref[...] = jnp.concatenate(outs, axis=-1).astype(o_ref.dtype)


def _ssd(xbc_act, dt_raw, z, b, s, dt_bias, a_log, d_skip, ssm_norm):
    L = SSM_CHUNK
    nc = s // L
    pad = LANES - DT_COLS
    dtb = jnp.pad(dt_bias.astype(F32).reshape(1, DT_COLS), ((0, 0), (0, pad)))
    a_neg = jnp.pad(-jnp.exp(a_log.astype(F32)).reshape(1, DT_COLS), ((0, 0), (0, pad)))
    dsk = d_skip.astype(F32)
    dsk = jnp.repeat(dsk[0] + dsk[1], SSM_HEADDIM).reshape(1, D_INNER)
    st_shape = pltpu.VMEM((SSM_HEADS, D_STATE, SSM_HEADDIM), F32)
    small = lambda w: pl.BlockSpec((1, w), lambda bi, ci: (0, 0))

    fwd_map = lambda bi, ci: (bi * nc + ci, 0)
    y_f = pl.pallas_call(
        functools.partial(_ssd_kernel, reverse=False),
        grid=(b, nc),
        in_specs=[pl.BlockSpec((L, CONV_DIM), fwd_map), pl.BlockSpec((L, LANES), fwd_map), small(LANES), small(LANES)],
        out_specs=pl.BlockSpec((L, D_INNER), fwd_map),
        out_shape=jax.ShapeDtypeStruct((b * s, D_INNER), F32),
        scratch_shapes=[st_shape],
        compiler_params=_cparams(("parallel", "arbitrary")),
        name="ssd_fwd",
    )(xbc_act, dt_raw, dtb, a_neg)

    bwd_map = lambda bi, ci: (bi * nc + (nc - 1 - ci), 0)
    return pl.pallas_call(
        functools.partial(_ssd_kernel, reverse=True),
        grid=(b, nc),
        in_specs=[
            pl.BlockSpec((L, CONV_DIM), bwd_map),
            pl.BlockSpec((L, LANES), bwd_map),
            small(LANES),
            small(LANES),
            pl.BlockSpec((L, D_INNER), bwd_map),
            pl.BlockSpec((L, D_INNER), bwd_map),
            small(D_INNER),
            small(D_INNER),
        ],
        out_specs=pl.BlockSpec((L, D_INNER), bwd_map),
        out_shape=jax.ShapeDtypeStruct((b * s, D_INNER), BF16),
        scratch_shapes=[st_shape],
        compiler_params=_cparams(("parallel", "arbitrary")),
        name="ssd_bwd",
    )(xbc_act, dt_raw, dtb, a_neg, y_f, z, dsk, ssm_norm.astype(F32).reshape(1, D_INNER))


def _cross_kernel(q_ref, k_ref, v_ref, o_ref):
    outs = []
    for h in range(CROSS_HEADS):
        sl = slice(h * CROSS_DIM, (h + 1) * CROSS_DIM)
        s = lax.dot_general(q_ref[:, sl], k_ref[:, sl], (((1,), (1,)), ((), ())), preferred_element_type=F32)
        s = s * (CROSS_DIM ** -0.5)
        p = jnp.exp(s - jnp.max(s, axis=-1, keepdims=True))
        p = p / jnp.sum(p, axis=-1, keepdims=True)
        outs.append(jnp.dot(p.astype(BF16), v_ref[:, sl], preferred_element_type=F32))
    o_ref[...] = jnp.concatenate(outs, axis=-1).astype(o_ref.dtype)


def _cross_core(q, kv, b, s, n_mem, tq=512):
    tq = _pick(s, tq)
    nq = s // tq
    return pl.pallas_call(
        _cross_kernel,
        grid=(b, nq),
        in_specs=[
            pl.BlockSpec((tq, D_MODEL), lambda bi, qi: (bi * nq + qi, 0)),
            pl.BlockSpec((n_mem, D_MODEL), lambda bi, qi: (bi, 0)),
            pl.BlockSpec((n_mem, D_MODEL), lambda bi, qi: (bi, 1)),
        ],
        out_specs=pl.BlockSpec((tq, D_MODEL), lambda bi, qi: (bi * nq + qi, 0)),
        out_shape=jax.ShapeDtypeStruct((b * s, D_MODEL), BF16),
        compiler_params=_cparams(("parallel", "parallel")),
        name="cross_attention",
    )(q, kv, kv)


def _top16(s, vals_scr, idx_scr, payload=None):
    n = s.shape[0]
    pos = lax.broadcasted_iota(jnp.int32, s.shape, 0).astype(F32)
    for k in range(PEER_TOPK):
        m = jnp.max(s, axis=0, keepdims=True)
        first = jnp.min(jnp.where(s == m, pos, float(n)), axis=0, keepdims=True)
        sel = pos == first
        vals_scr[k:k + 1, :] = m
        if payload is None:
            idx_scr[k:k + 1, :] = first
        else:
            idx_scr[k:k + 1, :] = jnp.max(jnp.where(sel, payload, -1.0), axis=0, keepdims=True)
        s = jnp.where(sel, -jnp.inf, s)


def _topk_kernel(q_ref, keys_ref, idx_ref, gate_ref, v1_scr, i1_scr, v2_scr, i2_scr, tv_scr, ti_scr):
    for h in range(PEER_HEADS):
        qh = q_ref[:, h * D_KEY:(h + 1) * D_KEY]
        sc = lax.dot_general(keys_ref[h], qh, (((1,), (1,)), ((), ())), preferred_element_type=F32)
        _top16(sc[0:N_KEYS], v1_scr, i1_scr)
        _top16(sc[N_KEYS:2 * N_KEYS], v2_scr, i2_scr)
        hs = SUBLANES
        pairs = [(slice(0, 1), slice(0, hs)), (slice(0, 1), slice(hs, 2 * hs))]
        pairs += [(slice(i, i + 1), slice(0, hs)) for i in range(1, hs)]
        pairs += [(slice(hs, 2 * hs), slice(0, 1))]
        cand = jnp.concatenate([v1_scr[a, :] + v2_scr[b, :] for a, b in pairs], axis=0)
        eidx = jnp.concatenate([i1_scr[a, :] * N_KEYS + i2_scr[b, :] for a, b in pairs], axis=0)
        _top16(cand, tv_scr, ti_scr, payload=eidx)
        top = tv_scr[...]
        e = jnp.exp(top - top[0:1, :])
        gate_ref[h * PEER_TOPK:(h + 1) * PEER_TOPK, :] = e / jnp.sum(e, axis=0, keepdims=True)
        idx_ref[h * PEER_TOPK:(h + 1) * PEER_TOPK, :] = ti_scr[...]


def _peer_topk(pq, keys_bd, tb=128):
    tt = pq.shape[0]
    nhk = PEER_HEADS * PEER_TOPK
    sc16 = lambda: pltpu.VMEM((PEER_TOPK, tb), F32)
    return pl.pallas_call(
        _topk_kernel,
        grid=(tt // tb,),
        in_specs=[
            pl.BlockSpec((tb, PEER_HEADS * D_KEY), lambda i: (i, 0)),
            pl.BlockSpec((PEER_HEADS, 2 * N_KEYS, D_KEY), lambda i: (0, 0, 0)),
        ],
        out_specs=[pl.BlockSpec((nhk, tb), lambda i: (0, i)), pl.BlockSpec((nhk, tb), lambda i: (0, i))],
        out_shape=[jax.ShapeDtypeStruct((nhk, tt), F32), jax.ShapeDtypeStruct((nhk, tt), F32)],
        scratch_shapes=[sc16() for _ in range(6)],
        compiler_params=_cparams(("parallel",)),
        name="peer_topk",
    )(pq, keys_bd)


GATE_GROUP = 16
GATE_PITCH = N_KEYS + 8


def _gate_kernel(idx_ref, gate_ref, o_ref, idx_scr, gate_scr, g_scr, *, tg):
    idx_scr[...] = idx_ref[...].T
    gate_scr[...] = gate_ref[...].T
    pos = lax.broadcasted_iota(jnp.int32, (N_KEYS, PEER_HEADS * PEER_TOPK), 0)
    half = GATE_GROUP // 2

    def body(gi, carry):
        base = pl.multiple_of(gi * GATE_GROUP, GATE_GROUP)
        for u in range(GATE_GROUP):
            e = idx_scr[pl.ds(base + u, 1), :].astype(jnp.int32)
            g = gate_scr[pl.ds(base + u, 1), :]
            a_t = jnp.where(pos == (e >> 7), 1.0, 0.0).astype(BF16)
            b_t = jnp.where(pos == (e & (N_KEYS - 1)), g, 0.0).astype(BF16)
            g_scr[u * GATE_PITCH:u * GATE_PITCH + N_KEYS, :] = lax.dot_general(
                a_t, b_t, (((1,), (1,)), ((), ())), preferred_element_type=F32)
        for e1 in range(N_KEYS):
            lo = g_scr[pl.ds(e1, half, stride=GATE_PITCH), :]
            hi = g_scr[pl.ds(half * GATE_PITCH + e1, half, stride=GATE_PITCH), :]
            o_ref[e1, pl.ds(base, GATE_GROUP), :] = jnp.concatenate([lo, hi], axis=0).astype(o_ref.dtype)
        return carry

    lax.fori_loop(0, tg // GATE_GROUP, body, 0)


def _peer_gates(idx, gate, tg=128):
    nhk, tt = idx.shape
    return pl.pallas_call(
        functools.partial(_gate_kernel, tg=tg),
        grid=(tt // tg,),
        in_specs=[pl.BlockSpec((nhk, tg), lambda i: (0, i)), pl.BlockSpec((nhk, tg), lambda i: (0, i))],
        out_specs=pl.BlockSpec((N_KEYS, tg, N_KEYS), lambda i: (0, i, 0)),
        out_shape=jax.ShapeDtypeStruct((N_KEYS, tt, N_KEYS), BF16),
        scratch_shapes=[
            pltpu.VMEM((tg, nhk), F32),
            pltpu.VMEM((tg, nhk), F32),
            pltpu.VMEM((GATE_GROUP * GATE_PITCH, N_KEYS), F32),
        ],
        compiler_params=_cparams(("parallel",)),
        name="peer_gates",
    )(idx, gate)


def _gelu_tanh(x):
    return x * (0.5 * (1.0 + jnp.tanh(math.sqrt(2.0 / math.pi) * (x + 0.044715 * (x * x * x)))))


def _ffn_kernel(h_ref, u_ref, v_ref, g_ref, x_ref, nf_ref, o_ref):
    j = pl.program_id(1)

    @pl.when(j == 0)
    def _():
        o_ref[...] = jnp.zeros(o_ref.shape, F32)

    s = lax.dot_general(h_ref[...], u_ref[...], (((1,), (1,)), ((), ())), preferred_element_type=F32)
    coeff = jnp.concatenate(
        [g_ref[k].astype(F32) * _gelu_tanh(s[:, k * LANES:(k + 1) * LANES]) for k in range(g_ref.shape[0])], axis=-1
    ).astype(BF16)
    o_ref[...] += jnp.dot(coeff, v_ref[...], preferred_element_type=F32)

    @pl.when(j == pl.num_programs(1) - 1)
    def _():
        x = x_ref[...] + o_ref[...]
        y = x * lax.rsqrt(jnp.mean(x * x, axis=-1, keepdims=True) + EPS)
        o_ref[...] = y * nf_ref[...]


def _peer_ffn(hp, u16, v16, gmat, x_res, norm_final, tm=512, te=1024):
    tt = hp.shape[0]
    tm = _pick(tt, tm)
    return pl.pallas_call(
        _ffn_kernel,
        grid=(tt // tm, N_EXPERTS // te),
        in_specs=[
            pl.BlockSpec((tm, D_MODEL), lambda i, j: (i, 0)),
            pl.BlockSpec((te, D_MODEL), lambda i, j: (j, 0)),
            pl.BlockSpec((te, D_MODEL), lambda i, j: (j, 0)),
            pl.BlockSpec((te // LANES, tm, LANES), lambda i, j: (j, i, 0)),
            pl.BlockSpec((tm, D_MODEL), lambda i, j: (i, 0)),
            pl.BlockSpec((1, D_MODEL), lambda i, j: (0, 0)),
        ],
        out_specs=pl.BlockSpec((tm, D_MODEL), lambda i, j: (i, 0)),
        out_shape=jax.ShapeDtypeStruct((tt, D_MODEL), F32),
        compiler_params=_cparams(("parallel", "arbitrary")),
        name="peer_ffn",
    )(hp, u16, v16, gmat, x_res, norm_final.reshape(1, D_MODEL).astype(F32))


def _prepare(p):
    w_in = p["w_in"][0]
    c = np.cumsum([0, 3 * Q_COLS, D_INNER, CONV_DIM, DT_COLS, 2 * D_MODEL])
    w = {}
    w["qkv"] = w_in[:, c[0]:c[1]].astype(BF16)
    w["z"] = w_in[:, c[1]:c[2]].astype(BF16)
    w["xbc"] = w_in[:, c[2]:c[3]].astype(BF16)
    w["dt"] = jnp.pad(w_in[:, c[3]:c[4]], ((0, 0), (0, LANES - DT_COLS))).astype(BF16)
    w["gates"] = w_in[:, c[4]:c[5]].astype(BF16)
    for name in ("w_attn_o", "w_ssm_o", "w_out", "w_cq", "w_ckv", "w_co", "w_pq", "expert_u", "expert_v"):
        w[name] = p[name][0].astype(BF16)
    sk = p["sub_keys"][0].astype(BF16)
    zero = jnp.zeros_like(sk[:, 0])
    w["keys_bd"] = jnp.concatenate(
        [jnp.concatenate([sk[:, 0], zero], axis=-1), jnp.concatenate([zero, sk[:, 1]], axis=-1)], axis=1
    )
    lam_init = 0.8 - 0.6 * math.exp(-0.3 * 0)
    w["lam"] = (
        jnp.exp(jnp.sum(p["lam_q1"][0].astype(F32) * p["lam_k1"][0].astype(F32)))
        - jnp.exp(jnp.sum(p["lam_q2"][0].astype(F32) * p["lam_k2"][0].astype(F32)))
        + lam_init
    )
    w["lam_init"] = lam_init
    return w


def _trunk(x, mem, p, w):
    b, s, d = x.shape
    n_mem = mem.shape[1]
    xf = x.reshape(b * s, d)

    h1 = _rmsnorm_bf16(xf, p["norm_mix"][0])
    qkv = _mm(h1, w["qkv"], BF16, name="mm_qkv")
    z = _mm(h1, w["z"], F32, name="mm_z")
    xbc = _mm(h1, w["xbc"], F32, tn=512, name="mm_xbc")
    dt_raw = _mm(h1, w["dt"], F32, name="mm_dt")
    gates = _mm(h1, w["gates"], F32, name="mm_gates")

    o_att = _diff_attention(qkv, b, s, p["rel_bias"], w["lam"], p["attn_subln"][0], 1.0 - w["lam_init"],
                            _pick(s, ATTN_TILE))
    xbc_act = _conv_silu(xbc, b, s, p["conv_w"][0], p["conv_b"][0])
    y_ssm = _ssd(xbc_act, dt_raw, z, b, s, p["dt_bias"][0], p["a_log"][0], p["d_skip"][0], p["ssm_norm"][0])

    t_att = _mm(o_att, w["w_attn_o"], F32, name="mm_attn_o")
    merged = _mm(y_ssm, w["w_ssm_o"], BF16, merge=(t_att, gates), name="mm_ssm_o_merge")
    x1 = _mm(merged, w["w_out"], F32, residual=xf, name="mm_out")

    hq = _rmsnorm_bf16(x1, p["norm_cross"][0])
    q = _mm(hq, w["w_cq"], BF16, name="mm_cq")
    mn = _rmsnorm_bf16(mem.reshape(b * n_mem, d), p["norm_mem"][0])
    kv = _mm(mn, w["w_ckv"], BF16, name="mm_ckv")
    oc = _cross_core(q, kv, b, s, n_mem)
    x2 = _mm(oc, w["w_co"], F32, residual=x1, name="mm_co")

    hp = _rmsnorm_bf16(x2, p["norm_ffn"][0])
    pq = _mm(hp, w["w_pq"], BF16, name="mm_pq")
    idx, gate = _peer_topk(pq, w["keys_bd"])
    gmat = _peer_gates(idx, gate)
    y = _peer_ffn(hp, w["expert_u"], w["expert_v"], gmat, x2, p["norm_final"])
    return y.reshape(b, s, d)


def kernel(x_prompt, x_sample, mem_prompt, mem_sample, norm_mix, w_in, lam_q1, lam_k1, lam_q2, lam_k2, rel_bias, attn_subln, w_attn_o, conv_w, conv_b, a_log, dt_bias, d_skip, ssm_norm, w_ssm_o, w_out, norm_cross, norm_mem, w_cq, w_ckv, w_co, norm_ffn, w_pq, sub_keys, expert_u, expert_v, norm_final):
    p = dict(norm_mix=norm_mix, w_in=w_in, lam_q1=lam_q1, lam_k1=lam_k1, lam_q2=lam_q2, lam_k2=lam_k2,
             rel_bias=rel_bias, attn_subln=attn_subln, w_attn_o=w_attn_o, conv_w=conv_w, conv_b=conv_b,
             a_log=a_log, dt_bias=dt_bias, d_skip=d_skip, ssm_norm=ssm_norm, w_ssm_o=w_ssm_o, w_out=w_out,
             norm_cross=norm_cross, norm_mem=norm_mem, w_cq=w_cq, w_ckv=w_ckv, w_co=w_co, norm_ffn=norm_ffn,
             w_pq=w_pq, sub_keys=sub_keys, expert_u=expert_u, expert_v=expert_v, norm_final=norm_final)
    w = _prepare(p)
    y_prompt = _trunk(x_prompt, mem_prompt, p, w)
    y_sample = _trunk(x_sample, mem_sample, p, w)
    return (y_prompt, y_sample)
```

```python
import functools
import math

import jax
import jax.numpy as jnp
import numpy as np
from jax import lax
from jax.experimental import pallas as pl
from jax.experimental.pallas import tpu as pltpu

F32 = jnp.float32
BF16 = jnp.bfloat16

D_MODEL = 2048
ATT_HEADS = 8
ATT_V_DIM = 128
ATT_QK_DIM = 64
N_BUCKETS = 32
MAX_DISTANCE = 128
D_INNER = 1024
SSM_HEADDIM = 64
SSM_HEADS = 16
SSM_GROUPS = 2
HEADS_PER_GROUP = 8
D_STATE = 128
D_CONV = 5
CONV_PAD = 2
CONV_DIM = D_INNER + 2 * SSM_GROUPS * D_STATE
SSM_CHUNK = 128
CROSS_HEADS = 4
CROSS_DIM = 512
PEER_HEADS = 8
N_KEYS = 128
N_EXPERTS = N_KEYS * N_KEYS
PEER_TOPK = 16
D_KEY = 128
D_KEY_HALF = 64
EPS = 1e-6

Q_COLS = 1024
V_COLS = 1024
DT_COLS = 32
LANES = 128
SUBLANES = 8
HALO = SUBLANES
ATTN_TILE = 512
ONES_ROWS = 16
ATTN_GROUP = 256
VMEM_LIMIT = 56 * 1024 * 1024


def _cparams(sem):
    return pltpu.CompilerParams(dimension_semantics=sem, vmem_limit_bytes=VMEM_LIMIT)


def _pick(n, pref):
    t = min(pref, n)
    while n % t:
        t //= 2
    return t


def _rmsnorm_kernel(x_ref, g_ref, o_ref):
    x = x_ref[...]
    y = x * lax.rsqrt(jnp.mean(x * x, axis=-1, keepdims=True) + EPS)
    o_ref[...] = (y * g_ref[...]).astype(o_ref.dtype)


def _rmsnorm_bf16(x, g):
    m, d = x.shape
    tm = _pick(m, 512)
    return pl.pallas_call(
        _rmsnorm_kernel,
        grid=(m // tm,),
        in_specs=[pl.BlockSpec((tm, d), lambda i: (i, 0)), pl.BlockSpec((1, d), lambda i: (0, 0))],
        out_specs=pl.BlockSpec((tm, d), lambda i: (i, 0)),
        out_shape=jax.ShapeDtypeStruct((m, d), BF16),
        compiler_params=_cparams(("parallel",)),
        name="rmsnorm_bf16",
    )(x, g.reshape(1, d).astype(F32))


def _mm_kernel(x_ref, w_ref, o_ref):
    o_ref[...] = jnp.dot(x_ref[...], w_ref[...], preferred_element_type=F32).astype(o_ref.dtype)


def _mm_residual_kernel(x_ref, w_ref, r_ref, o_ref):
    acc = jnp.dot(x_ref[...], w_ref[...], preferred_element_type=F32)
    o_ref[...] = (r_ref[...] + acc).astype(o_ref.dtype)


def _mm_merge_kernel(x_ref, w_ref, t_ref, g0_ref, g1_ref, o_ref):
    acc = jnp.dot(x_ref[...], w_ref[...], preferred_element_type=F32)
    g0 = jax.nn.sigmoid(g0_ref[...])
    g1 = jax.nn.sigmoid(g1_ref[...])
    o_ref[...] = (g0 * t_ref[...] + g1 * acc).astype(o_ref.dtype)


def _mm(x, w, out_dtype, *, residual=None, merge=None, tm=1024, tn=512, name="mm"):
    m, k = x.shape
    n = w.shape[1]
    tm = _pick(m, tm)
    tn = _pick(n, tn)
    in_specs = [pl.BlockSpec((tm, k), lambda i, j: (i, 0)), pl.BlockSpec((k, tn), lambda i, j: (0, j))]
    args = [x, w]
    kern = _mm_kernel
    if residual is not None:
        kern = _mm_residual_kernel
        in_specs.append(pl.BlockSpec((tm, tn), lambda i, j: (i, j)))
        args.append(residual)
    if merge is not None:
        kern = _mm_merge_kernel
        t_att, gates = merge
        nb = n // tn
        in_specs += [
            pl.BlockSpec((tm, tn), lambda i, j: (i, j)),
            pl.BlockSpec((tm, tn), lambda i, j: (i, j)),
            pl.BlockSpec((tm, tn), lambda i, j: (i, j + nb)),
        ]
        args += [t_att, gates, gates]
    return pl.pallas_call(
        kern,
        grid=(m // tm, n // tn),
        in_specs=in_specs,
        out_specs=pl.BlockSpec((tm, tn), lambda i, j: (i, j)),
        out_shape=jax.ShapeDtypeStruct((m, n), out_dtype),
        compiler_params=_cparams(("parallel", "parallel")),
        name=name,
    )(*args)


def _t5_bucket(rel):
    half = N_BUCKETS // 2
    exact = half // 2
    n = jnp.abs(rel)
    far = exact + (
        jnp.log(jnp.maximum(n, 1).astype(F32) / exact) / math.log(MAX_DISTANCE / exact) * (half - exact)
    ).astype(jnp.int32)
    far = jnp.minimum(far, half - 1)
    return jnp.where(rel > 0, half, 0) + jnp.where(n < exact, n, far)


def _attn_kernel(lam_ref, far_ref, q_ref, k_ref, vt_ref, bias_ref, g_ref, o_ref, q2_scr, sa_scr, sb_scr, m_scr, acc_scr, *, t,
                 nk, out_scale):
    h = pl.program_id(1)
    qi = pl.program_id(2)
    qt = (q_ref[...].astype(F32) * (ATT_QK_DIM ** -0.5)).T
    row = lax.broadcasted_iota(jnp.int32, qt.shape, 0)
    q2_scr[:, 0:t] = jnp.where(row < ATT_QK_DIM, qt, 0.0).astype(BF16)
    q2_scr[:, t:2 * t] = jnp.where(row >= ATT_QK_DIM, qt, 0.0).astype(BF16)
    m_scr[...] = jnp.full(m_scr.shape, -jnp.inf, F32)
    acc_scr[...] = jnp.zeros(acc_scr.shape, F32)

    gw = ATTN_GROUP
    ngroups = 2 * t // gw

    def scores(kc, g):
        start = pl.multiple_of(kc * t, t)
        return jnp.dot(k_ref[pl.ds(start, t), :], q2_scr[:, g * gw:(g + 1) * gw],
                       preferred_element_type=F32)

    def update(g, st, vt, const, tile_idx):
        cols = slice(g * gw, (g + 1) * gw)
        if tile_idx is not None:
            b0 = (g * gw) % t
            st = st + bias_ref[tile_idx, :, b0:b0 + gw]
        m_cur = jnp.max(st, axis=0, keepdims=True)
        if const is not None:
            m_cur = m_cur + const
        m_old = m_scr[:, cols]
        m_new = jnp.maximum(m_old, m_cur)
        alpha = jnp.exp(m_old - m_new)
        shift = -m_new if const is None else const - m_new
        pt = jnp.exp(st + shift)
        acc_scr[:, cols] = alpha * acc_scr[:, cols] + jnp.dot(vt, pt.astype(BF16), preferred_element_type=F32)
        m_scr[:, cols] = m_new

    lo_end = jnp.maximum(qi - 1, 0)
    near_end = jnp.minimum(qi + 2, nk)
    three = (near_end - lo_end) == 3
    t_start = lo_end - jnp.where(three & (qi + 2 > nk - 1), 1, 0)
    n_tile = jnp.where(three, 4, 2)
    n_const = nk - n_tile

    def tile_chunk(j):
        kc = jnp.minimum(t_start + j, nk - 1)
        rel = kc - qi
        return kc, jnp.where(rel < -1, 3, jnp.where(rel > 1, 4, rel + 1))

    def const_chunk(f):
        f = jnp.minimum(f, n_const - 1)
        lo = f < t_start
        return jnp.where(lo, f, f + n_tile), jnp.where(lo, far_ref[h, 0], far_ref[h, 1])

    def pipeline(n_chunks, chunk_of, use_tile):
        @pl.when(n_chunks >= 2)
        def _():
            k0, _ = chunk_of(0)
            for g in range(ngroups):
                sa_scr[:, g * gw:(g + 1) * gw] = scores(k0, g)

        def pair_body(j, carry):
            (ka, xa), (kb, xb), (kn, _) = chunk_of(2 * j), chunk_of(2 * j + 1), chunk_of(2 * j + 2)
            for src, dst, k_cur, k_nxt, x in ((sa_scr, sb_scr, ka, kb, xa), (sb_scr, sa_scr, kb, kn, xb)):
                vt = vt_ref[k_cur]
                for g in range(ngroups):
                    cols = slice(g * gw, (g + 1) * gw)
                    dst[:, cols] = scores(k_nxt, g)
                    if use_tile:
                        update(g, src[:, cols], vt, None, x)
                    else:
                        update(g, src[:, cols], vt, x, None)
            return carry

        lax.fori_loop(0, n_chunks // 2, pair_body, 0)

    pipeline(n_tile, tile_chunk, True)
    pipeline(n_const, const_chunk, False)

    lam = lam_ref[0]
    dv = ATT_V_DIM
    ot = (acc_scr[0:dv, 0:t] / acc_scr[dv:dv + 1, 0:t]
          - lam * (acc_scr[0:dv, t:2 * t] / acc_scr[dv:dv + 1, t:2 * t]))
    o = ot.T
    y = o * lax.rsqrt(jnp.mean(o * o, axis=-1, keepdims=True) + EPS)
    o_ref[...] = ((y * g_ref[...]) * out_scale).astype(o_ref.dtype)


def _diff_attention(qkv, b, s, rel_bias, lam, subln, out_scale, t):
    nk = s // t
    hq = ATT_HEADS
    r = jnp.arange(t, dtype=jnp.int32)
    rel = (jnp.arange(3, dtype=jnp.int32)[:, None, None] - 1) * t + r[None, :, None] - r[None, None, :]
    bucket = _t5_bucket(rel)[None]
    rb32 = rel_bias.astype(F32)
    bias_tiles = jnp.zeros((hq, 3, t, t), F32)
    for n in range(N_BUCKETS):
        bias_tiles = jnp.where(bucket == n, rb32[n][:, None, None, None], bias_tiles)
    half = N_BUCKETS // 2
    far = jnp.stack([rel_bias[half - 1], rel_bias[N_BUCKETS - 1]], axis=-1).astype(F32)
    bias_tiles = jnp.concatenate([bias_tiles, jnp.broadcast_to(far[:, :, None, None], (hq, 2, t, t))], axis=1)
    assert nk % 2 == 0 and t % ATTN_GROUP == 0
    vt = qkv[:, 2 * Q_COLS:].reshape(b, nk, t, hq, ATT_V_DIM).transpose(0, 3, 1, 4, 2)
    vt = jnp.concatenate([vt, jnp.ones((b, hq, nk, ONES_ROWS, t), BF16)], axis=3)
    kern = functools.partial(_attn_kernel, t=t, nk=nk, out_scale=out_scale)
    return pl.pallas_call(
        kern,
        grid=(b, hq, nk),
        in_specs=[
            pl.BlockSpec(memory_space=pltpu.SMEM),
            pl.BlockSpec(memory_space=pltpu.SMEM),
            pl.BlockSpec((t, LANES), lambda bi, h, qi: (bi * nk + qi, h)),
            pl.BlockSpec((s, LANES), lambda bi, h, qi: (bi, hq + h)),
            pl.BlockSpec((None, None, nk, ATT_V_DIM + ONES_ROWS, t), lambda bi, h, qi: (bi, h, 0, 0, 0)),
            pl.BlockSpec((None, 5, t, t), lambda bi, h, qi: (h, 0, 0, 0)),
            pl.BlockSpec((1, LANES), lambda bi, h, qi: (0, 0)),
        ],
        out_specs=pl.BlockSpec((t, LANES), lambda bi, h, qi: (bi * nk + qi, h)),
        out_shape=jax.ShapeDtypeStruct((b * s, V_COLS), BF16),
        scratch_shapes=[
            pltpu.VMEM((LANES, 2 * t), BF16),
            pltpu.VMEM((t, 2 * t), F32),
            pltpu.VMEM((t, 2 * t), F32),
            pltpu.VMEM((1, 2 * t), F32),
            pltpu.VMEM((ATT_V_DIM + ONES_ROWS, 2 * t), F32),
        ],
        compiler_params=_cparams(("parallel", "parallel", "parallel")),
        name="diff_attention",
    )(lam.reshape(1).astype(F32), far, qkv, qkv, vt, bias_tiles, subln.reshape(1, LANES).astype(F32))


def _conv_kernel(prev_ref, cur_ref, next_ref, w_ref, b_ref, o_ref, ext_scr, *, tb, ns):
    si = pl.program_id(1)
    ext_scr[0:HALO, :] = jnp.where(si > 0, prev_ref[...], 0.0)
    ext_scr[HALO:HALO + tb, :] = cur_ref[...]
    ext_scr[HALO + tb:2 * HALO + tb, :] = jnp.where(si < ns - 1, next_ref[...], 0.0)
    y = b_ref[...]
    for j in range(D_CONV):
        off = HALO - CONV_PAD + j
        y = y + w_ref[j:j + 1, :] * ext_scr[off:off + tb, :]
    o_ref[...] = y * jax.nn.sigmoid(y)


def _conv_silu(xbc, b, s, conv_w, conv_b, tb=1024, tc=CONV_DIM // 2):
    tb = _pick(s, tb)
    ns = s // tb
    hb = tb // HALO
    nc = CONV_DIM // tc
    kern = functools.partial(_conv_kernel, tb=tb, ns=ns)
    return pl.pallas_call(
        kern,
        grid=(b, ns, nc),
        in_specs=[
            pl.BlockSpec((HALO, tc), lambda bi, si, ci: (jnp.maximum((bi * ns + si) * hb - 1, 0), ci)),
            pl.BlockSpec((tb, tc), lambda bi, si, ci: (bi * ns + si, ci)),
            pl.BlockSpec((HALO, tc), lambda bi, si, ci: (jnp.minimum((bi * ns + si + 1) * hb, b * ns * hb - 1), ci)),
            pl.BlockSpec((D_CONV, tc), lambda bi, si, ci: (0, ci)),
            pl.BlockSpec((1, tc), lambda bi, si, ci: (0, ci)),
        ],
        out_specs=pl.BlockSpec((tb, tc), lambda bi, si, ci: (bi * ns + si, ci)),
        out_shape=jax.ShapeDtypeStruct(xbc.shape, F32),
        scratch_shapes=[pltpu.VMEM((tb + 2 * HALO, tc), F32)],
        compiler_params=_cparams(("parallel", "parallel", "parallel")),
        name="conv_silu",
    )(xbc, xbc, xbc, conv_w.astype(F32), conv_b.reshape(1, CONV_DIM).astype(F32))


def _softplus(x):
    return jnp.maximum(x, 0.0) + jnp.log1p(jnp.exp(-jnp.abs(x)))


def _ssd_kernel(*refs, reverse):
    if reverse:
        x_ref, dt_ref, dtb_ref, a_ref, yf_ref, z_ref, dsk_ref, gn_ref, o_ref, st_scr = refs
    else:
        x_ref, dt_ref, dtb_ref, a_ref, o_ref, st_scr = refs
    L = SSM_CHUNK
    d = 1 if reverse else 0

    @pl.when(pl.program_id(1) == 0)
    def _():
        st_scr[...] = jnp.zeros(st_scr.shape, F32)

    dt = _softplus(dt_ref[...] + dtb_ref[...])
    a = dt * a_ref[...]
    row = lax.broadcasted_iota(jnp.int32, (L, L), 0)
    col = lax.broadcasted_iota(jnp.int32, (L, L), 1)
    tril = (row >= col).astype(F32)
    acs = jnp.dot(tril, a, preferred_element_type=F32, precision=lax.Precision.HIGHEST)
    ecs = acs - a if reverse else acs
    ecs_t = ecs.T
    keep = (col >= row) if reverse else (row >= col)
    total = jnp.broadcast_to(acs[L - 1:L, :], (SUBLANES, LANES))
    if reverse:
        out_decay = jnp.exp(total[0:1, :] - ecs)
        in_decay = jnp.exp(ecs)
    else:
        out_decay = jnp.exp(ecs)
        in_decay = jnp.exp(total[0:1, :] - ecs)

    ch_head = lax.broadcasted_iota(jnp.int32, (LANES, D_INNER), 1) // SSM_HEADDIM + d * SSM_HEADS
    spread = jnp.where(lax.broadcasted_iota(jnp.int32, (LANES, D_INNER), 0) == ch_head, 1.0, 0.0).astype(BF16)

    def per_channel(v):
        hi = v.astype(BF16)
        lo = (v - hi.astype(F32)).astype(BF16)
        return (jnp.dot(hi, spread, preferred_element_type=F32) + jnp.dot(lo, spread, preferred_element_type=F32))

    xdt = x_ref[:, 0:D_INNER] * per_channel(dt)
    xdt16 = xdt.astype(BF16)
    xin16 = (xdt * per_channel(in_decay)).astype(BF16)
    out_ch = per_channel(out_decay)
    tot_ch = per_channel(jnp.exp(total))[0:1, :]
    first = lax.broadcasted_iota(jnp.int32, (L, LANES), 1) < SSM_HEADDIM

    ys = []
    for g in range(SSM_GROUPS):
        bg = x_ref[:, D_INNER + g * D_STATE:D_INNER + (g + 1) * D_STATE]
        cg = x_ref[:, D_INNER + SSM_GROUPS * D_STATE + g * D_STATE:D_INNER + SSM_GROUPS * D_STATE + (g + 1) * D_STATE]
        cg16 = cg.astype(BF16)
        cb = lax.dot_general(cg16, bg.astype(BF16), (((1,), (1,)), ((), ())), preferred_element_type=F32)
        bgt16 = bg.T.astype(BF16)
        for pr in range(HEADS_PER_GROUP // 2):
            pair = g * (HEADS_PER_GROUP // 2) + pr
            lanes = slice(pair * LANES, (pair + 1) * LANES)
            xp = xdt16[:, lanes]
            halves = (jnp.where(first, xp, jnp.zeros_like(xp)), jnp.where(first, jnp.zeros_like(xp), xp))
            y = None
            for k in range(2):
                c = d * SSM_HEADS + 2 * pair + k
                e_col = ecs[:, c:c + 1]
                e_row = ecs_t[c:c + 1, :]
                diff = e_row - e_col if reverse else e_col - e_row
                seg = jnp.exp(jnp.where(keep, diff, -jnp.inf))
                yk = jnp.dot((seg * cb).astype(BF16), halves[k], preferred_element_type=F32)
                y = yk if y is None else y + yk
            st = st_scr[pair]
            y = y + jnp.dot(cg16, st.astype(BF16), preferred_element_type=F32) * out_ch[:, lanes]
            st_scr[pair] = st * tot_ch[:, lanes] + jnp.dot(bgt16, xin16[:, lanes], preferred_element_type=F32)
            ys.append(y)
    y = jnp.concatenate(ys, axis=-1)
    if not reverse:
        o_ref[...] = y
        return
    xs = x_ref[:, 0:D_INNER]
    y = y + yf_ref[...] + dsk_ref[...] * xs
    z = z_ref[...]
    y = y * (z * jax.nn.sigmoid(z))
    gw = D_INNER // SSM_GROUPS
    outs = []
    for g in range(SSM_GROUPS):
        yg = y[:, g * gw:(g + 1) * gw]
        yn = yg * lax.rsqrt(jnp.mean(yg * yg, axis=-1, keepdims=True) + EPS)
        outs.append(yn * gn_ref[:, g * gw:(g + 1) * gw])
    o_ref[...] = jnp.concatenate(outs, axis=-1).astype(o_ref.dtype)


def _ssd(xbc_act, dt_raw, z, b, s, dt_bias, a_log, d_skip, ssm_norm):
    L = SSM_CHUNK
    nc = s // L
    pad = LANES - DT_COLS
    dtb = jnp.pad(dt_bias.astype(F32).reshape(1, DT_COLS), ((0, 0), (0, pad)))
    a_neg = jnp.pad(-jnp.exp(a_log.astype(F32)).reshape(1, DT_COLS), ((0, 0), (0, pad)))
    dsk = d_skip.astype(F32)
    dsk = jnp.repeat(dsk[0] + dsk[1], SSM_HEADDIM).reshape(1, D_INNER)
    st_shape = pltpu.VMEM((SSM_HEADS // 2, D_STATE, 2 * SSM_HEADDIM), F32)
    small = lambda w: pl.BlockSpec((1, w), lambda bi, ci: (0, 0))

    fwd_map = lambda bi, ci: (bi * nc + ci, 0)
    y_f = pl.pallas_call(
        functools.partial(_ssd_kernel, reverse=False),
        grid=(b, nc),
        in_specs=[pl.BlockSpec((L, CONV_DIM), fwd_map), pl.BlockSpec((L, LANES), fwd_map), small(LANES), small(LANES)],
        out_specs=pl.BlockSpec((L, D_INNER), fwd_map),
        out_shape=jax.ShapeDtypeStruct((b * s, D_INNER), F32),
        scratch_shapes=[st_shape],
        compiler_params=_cparams(("parallel", "arbitrary")),
        name="ssd_fwd",
    )(xbc_act, dt_raw, dtb, a_neg)

    bwd_map = lambda bi, ci: (bi * nc + (nc - 1 - ci), 0)
    return pl.pallas_call(
        functools.partial(_ssd_kernel, reverse=True),
        grid=(b, nc),
        in_specs=[
            pl.BlockSpec((L, CONV_DIM), bwd_map),
            pl.BlockSpec((L, LANES), bwd_map),
            small(LANES),
            small(LANES),
            pl.BlockSpec((L, D_INNER), bwd_map),
            pl.BlockSpec((L, D_INNER), bwd_map),
            small(D_INNER),
            small(D_INNER),
        ],
        out_specs=pl.BlockSpec((L, D_INNER), bwd_map),
        out_shape=jax.ShapeDtypeStruct((b * s, D_INNER), BF16),
        scratch_shapes=[st_shape],
        compiler_params=_cparams(("parallel", "arbitrary")),
        name="ssd_bwd",
    )(xbc_act, dt_raw, dtb, a_neg, y_f, z, dsk, ssm_norm.astype(F32).reshape(1, D_INNER))


def _cross_kernel(q_ref, k_ref, v_ref, o_ref):
    outs = []
    for h in range(CROSS_HEADS):
        sl = slice(h * CROSS_DIM, (h + 1) * CROSS_DIM)
        s = lax.dot_general(q_ref[:, sl], k_ref[:, sl], (((1,), (1,)), ((), ())), preferred_element_type=F32)
        s = s * (CROSS_DIM ** -0.5)
        p = jnp.exp(s - jnp.max(s, axis=-1, keepdims=True))
        p = p / jnp.sum(p, axis=-1, keepdims=True)
        outs.append(jnp.dot(p.astype(BF16), v_ref[:, sl], preferred_element_type=F32))
    o_ref[...] = jnp.concatenate(outs, axis=-1).astype(o_ref.dtype)


def _cross_core(q, kv, b, s, n_mem, tq=512):
    tq = _pick(s, tq)
    nq = s // tq
    return pl.pallas_call(
        _cross_kernel,
        grid=(b, nq),
        in_specs=[
            pl.BlockSpec((tq, D_MODEL), lambda bi, qi: (bi * nq + qi, 0)),
            pl.BlockSpec((n_mem, D_MODEL), lambda bi, qi: (bi, 0)),
            pl.BlockSpec((n_mem, D_MODEL), lambda bi, qi: (bi, 1)),
        ],
        out_specs=pl.BlockSpec((tq, D_MODEL), lambda bi, qi: (bi * nq + qi, 0)),
        out_shape=jax.ShapeDtypeStruct((b * s, D_MODEL), BF16),
        compiler_params=_cparams(("parallel", "parallel")),
        name="cross_attention",
    )(q, kv, kv)


def _top16(s, vals_scr, idx_scr, payload=None):
    n = s.shape[0]
    pos = lax.broadcasted_iota(jnp.int32, s.shape, 0).astype(F32)
    for k in range(PEER_TOPK):
        m = jnp.max(s, axis=0, keepdims=True)
        first = jnp.min(jnp.where(s == m, pos, float(n)), axis=0, keepdims=True)
        sel = pos == first
        vals_scr[k:k + 1, :] = m
        if payload is None:
            idx_scr[k:k + 1, :] = first
        else:
            idx_scr[k:k + 1, :] = jnp.max(jnp.where(sel, payload, -1.0), axis=0, keepdims=True)
        s = jnp.where(sel, -jnp.inf, s)


def _topk_kernel(q_ref, keys_ref, idx_ref, gate_ref, v1_scr, i1_scr, v2_scr, i2_scr, tv_scr, ti_scr):
    for h in range(PEER_HEADS):
        qh = q_ref[:, h * D_KEY:(h + 1) * D_KEY]
        sc = lax.dot_general(keys_ref[h], qh, (((1,), (1,)), ((), ())), preferred_element_type=F32)
        _top16(sc[0:N_KEYS], v1_scr, i1_scr)
        _top16(sc[N_KEYS:2 * N_KEYS], v2_scr, i2_scr)
        hs = SUBLANES
        pairs = [(slice(0, 1), slice(0, hs)), (slice(0, 1), slice(hs, 2 * hs))]
        pairs += [(slice(i, i + 1), slice(0, hs)) for i in range(1, hs)]
        pairs += [(slice(hs, 2 * hs), slice(0, 1))]
        cand = jnp.concatenate([v1_scr[a, :] + v2_scr[b, :] for a, b in pairs], axis=0)
        eidx = jnp.concatenate([i1_scr[a, :] * N_KEYS + i2_scr[b, :] for a, b in pairs], axis=0)
        _top16(cand, tv_scr, ti_scr, payload=eidx)
        top = tv_scr[...]
        e = jnp.exp(top - top[0:1, :])
        gate_ref[h * PEER_TOPK:(h + 1) * PEER_TOPK, :] = e / jnp.sum(e, axis=0, keepdims=True)
        idx_ref[h * PEER_TOPK:(h + 1) * PEER_TOPK, :] = ti_scr[...]


def _peer_topk(pq, keys_bd, tb=128):
    tt = pq.shape[0]
    nhk = PEER_HEADS * PEER_TOPK
    sc16 = lambda: pltpu.VMEM((PEER_TOPK, tb), F32)
    return pl.pallas_call(
        _topk_kernel,
        grid=(tt // tb,),
        in_specs=[
            pl.BlockSpec((tb, PEER_HEADS * D_KEY), lambda i: (i, 0)),
            pl.BlockSpec((PEER_HEADS, 2 * N_KEYS, D_KEY), lambda i: (0, 0, 0)),
        ],
        out_specs=[pl.BlockSpec((nhk, tb), lambda i: (0, i)), pl.BlockSpec((nhk, tb), lambda i: (0, i))],
        out_shape=[jax.ShapeDtypeStruct((nhk, tt), F32), jax.ShapeDtypeStruct((nhk, tt), F32)],
        scratch_shapes=[sc16() for _ in range(6)],
        compiler_params=_cparams(("parallel",)),
        name="peer_topk",
    )(pq, keys_bd)


GATE_GROUP = 16
GATE_PITCH = N_KEYS + 8


def _gate_kernel(idx_ref, gate_ref, o_ref, idx_scr, gate_scr, ga_scr, gb_scr, *, tg):
    idx_scr[...] = idx_ref[...].T
    gate_scr[...] = gate_ref[...].T
    pos = lax.broadcasted_iota(jnp.int32, (N_KEYS, PEER_HEADS * PEER_TOPK), 0)
    half = GATE_GROUP // 2
    ngroups = tg // GATE_GROUP

    def tiles(gi, dst):
        base = pl.multiple_of(gi * GATE_GROUP, GATE_GROUP)
        for u in range(GATE_GROUP):
            e = idx_scr[pl.ds(base + u, 1), :].astype(jnp.int32)
            g = gate_scr[pl.ds(base + u, 1), :]
            a_t = jnp.where(pos == (e >> 7), 1.0, 0.0).astype(BF16)
            b_t = jnp.where(pos == (e & (N_KEYS - 1)), g, 0.0).astype(BF16)
            dst[u * GATE_PITCH:u * GATE_PITCH + N_KEYS, :] = lax.dot_general(
                a_t, b_t, (((1,), (1,)), ((), ())), preferred_element_type=F32)

    def relayout(gi, src):
        base = pl.multiple_of(gi * GATE_GROUP, GATE_GROUP)
        for e1 in range(N_KEYS):
            lo = src[pl.ds(e1, half, stride=GATE_PITCH), :]
            hi = src[pl.ds(half * GATE_PITCH + e1, half, stride=GATE_PITCH), :]
            o_ref[e1, pl.ds(base, GATE_GROUP), :] = jnp.concatenate([lo, hi], axis=0).astype(o_ref.dtype)

    tiles(0, ga_scr)

    def body(j, carry):
        tiles(2 * j + 1, gb_scr)
        relayout(2 * j, ga_scr)
        tiles(jnp.minimum(2 * j + 2, ngroups - 1), ga_scr)
        relayout(2 * j + 1, gb_scr)
        return carry

    lax.fori_loop(0, ngroups // 2, body, 0)


def _peer_gates(idx, gate, tg=256):
    nhk, tt = idx.shape
    return pl.pallas_call(
        functools.partial(_gate_kernel, tg=tg),
        grid=(tt // tg,),
        in_specs=[pl.BlockSpec((nhk, tg), lambda i: (0, i)), pl.BlockSpec((nhk, tg), lambda i: (0, i))],
        out_specs=pl.BlockSpec((N_KEYS, tg, N_KEYS), lambda i: (0, i, 0)),
        out_shape=jax.ShapeDtypeStruct((N_KEYS, tt, N_KEYS), BF16),
        scratch_shapes=[
            pltpu.VMEM((tg, nhk), F32),
            pltpu.VMEM((tg, nhk), F32),
            pltpu.VMEM((GATE_GROUP * GATE_PITCH, N_KEYS), F32),
            pltpu.VMEM((GATE_GROUP * GATE_PITCH, N_KEYS), F32),
        ],
        compiler_params=_cparams(("parallel",)),
        name="peer_gates",
    )(idx, gate)


def _gelu_tanh(x):
    return x * (0.5 * (1.0 + jnp.tanh(math.sqrt(2.0 / math.pi) * (x + 0.044715 * (x * x * x)))))


def _ffn_kernel(h_ref, u_ref, v_ref, g_ref, x_ref, nf_ref, o_ref):
    j = pl.program_id(1)

    @pl.when(j == 0)
    def _():
        o_ref[...] = jnp.zeros(o_ref.shape, F32)

    s = lax.dot_general(h_ref[...], u_ref[...], (((1,), (1,)), ((), ())), preferred_element_type=F32)
    coeff = jnp.concatenate(
        [g_ref[k].astype(F32) * _gelu_tanh(s[:, k * LANES:(k + 1) * LANES]) for k in range(g_ref.shape[0])], axis=-1
    ).astype(BF16)
    o_ref[...] += jnp.dot(coeff, v_ref[...], preferred_element_type=F32)

    @pl.when(j == pl.num_programs(1) - 1)
    def _():
        x = x_ref[...] + o_ref[...]
        y = x * lax.rsqrt(jnp.mean(x * x, axis=-1, keepdims=True) + EPS)
        o_ref[...] = y * nf_ref[...]


def _peer_ffn(hp, u16, v16, gmat, x_res, norm_final, tm=512, te=1024):
    tt = hp.shape[0]
    tm = _pick(tt, tm)
    return pl.pallas_call(
        _ffn_kernel,
        grid=(tt // tm, N_EXPERTS // te),
        in_specs=[
            pl.BlockSpec((tm, D_MODEL), lambda i, j: (i, 0)),
            pl.BlockSpec((te, D_MODEL), lambda i, j: (j, 0)),
            pl.BlockSpec((te, D_MODEL), lambda i, j: (j, 0)),
            pl.BlockSpec((te // LANES, tm, LANES), lambda i, j: (j, i, 0)),
            pl.BlockSpec((tm, D_MODEL), lambda i, j: (i, 0)),
            pl.BlockSpec((1, D_MODEL), lambda i, j: (0, 0)),
        ],
        out_specs=pl.BlockSpec((tm, D_MODEL), lambda i, j: (i, 0)),
        out_shape=jax.ShapeDtypeStruct((tt, D_MODEL), F32),
        compiler_params=_cparams(("parallel", "arbitrary")),
        name="peer_ffn",
    )(hp, u16, v16, gmat, x_res, norm_final.reshape(1, D_MODEL).astype(F32))


def _prepare(p):
    w_in = p["w_in"][0]
    c = np.cumsum([0, 3 * Q_COLS, D_INNER, CONV_DIM, DT_COLS, 2 * D_MODEL])
    w = {}
    w["qkv"] = w_in[:, c[0]:c[1]].astype(BF16)
    w["z"] = w_in[:, c[1]:c[2]].astype(BF16)
    w["xbc"] = w_in[:, c[2]:c[3]].astype(BF16)
    w["dt"] = jnp.pad(w_in[:, c[3]:c[4]], ((0, 0), (0, LANES - DT_COLS))).astype(BF16)
    w["gates"] = w_in[:, c[4]:c[5]].astype(BF16)
    for name in ("w_attn_o", "w_ssm_o", "w_out", "w_cq", "w_ckv", "w_co", "w_pq", "expert_u", "expert_v"):
        w[name] = p[name][0].astype(BF16)
    sk = p["sub_keys"][0].astype(BF16)
    zero = jnp.zeros_like(sk[:, 0])
    w["keys_bd"] = jnp.concatenate(
        [jnp.concatenate([sk[:, 0], zero], axis=-1), jnp.concatenate([zero, sk[:, 1]], axis=-1)], axis=1
    )
    lam_init = 0.8 - 0.6 * math.exp(-0.3 * 0)
    w["lam"] = (
        jnp.exp(jnp.sum(p["lam_q1"][0].astype(F32) * p["lam_k1"][0].astype(F32)))
        - jnp.exp(jnp.sum(p["lam_q2"][0].astype(F32) * p["lam_k2"][0].astype(F32)))
        + lam_init
    )
    w["lam_init"] = lam_init
    return w


def _trunk(x, mem, p, w):
    b, s, d = x.shape
    n_mem = mem.shape[1]
    xf = x.reshape(b * s, d)

    h1 = _rmsnorm_bf16(xf, p["norm_mix"][0])
    qkv = _mm(h1, w["qkv"], BF16, name="mm_qkv")
    z = _mm(h1, w["z"], F32, name="mm_z")
    xbc = _mm(h1, w["xbc"], F32, tn=512, name="mm_xbc")
    dt_raw = _mm(h1, w["dt"], F32, name="mm_dt")
    gates = _mm(h1, w["gates"], F32, name="mm_gates")

    o_att = _diff_attention(qkv, b, s, p["rel_bias"], w["lam"], p["attn_subln"][0], 1.0 - w["lam_init"],
                            _pick(s, ATTN_TILE))
    xbc_act = _conv_silu(xbc, b, s, p["conv_w"][0], p["conv_b"][0])
    y_ssm = _ssd(xbc_act, dt_raw, z, b, s, p["dt_bias"][0], p["a_log"][0], p["d_skip"][0], p["ssm_norm"][0])

    t_att = _mm(o_att, w["w_attn_o"], F32, name="mm_attn_o")
    merged = _mm(y_ssm, w["w_ssm_o"], BF16, merge=(t_att, gates), name="mm_ssm_o_merge")
    x1 = _mm(merged, w["w_out"], F32, residual=xf, name="mm_out")

    hq = _rmsnorm_bf16(x1, p["norm_cross"][0])
    q = _mm(hq, w["w_cq"], BF16, name="mm_cq")
    mn = _rmsnorm_bf16(mem.reshape(b * n_mem, d), p["norm_mem"][0])
    kv = _mm(mn, w["w_ckv"], BF16, name="mm_ckv")
    oc = _cross_core(q, kv, b, s, n_mem)
    x2 = _mm(oc, w["w_co"], F32, residual=x1, name="mm_co")

    hp = _rmsnorm_bf16(x2, p["norm_ffn"][0])
    pq = _mm(hp, w["w_pq"], BF16, name="mm_pq")
    idx, gate = _peer_topk(pq, w["keys_bd"])
    gmat = _peer_gates(idx, gate)
    y = _peer_ffn(hp, w["expert_u"], w["expert_v"], gmat, x2, p["norm_final"])
    return y.reshape(b, s, d)


def kernel(x_prompt, x_sample, mem_prompt, mem_sample, norm_mix, w_in, lam_q1, lam_k1, lam_q2, lam_k2, rel_bias, attn_subln, w_attn_o, conv_w, conv_b, a_log, dt_bias, d_skip, ssm_norm, w_ssm_o, w_out, norm_cross, norm_mem, w_cq, w_ckv, w_co, norm_ffn, w_pq, sub_keys, expert_u, expert_v, norm_final):
    p = dict(norm_mix=norm_mix, w_in=w_in, lam_q1=lam_q1, lam_k1=lam_k1, lam_q2=lam_q2, lam_k2=lam_k2,
             rel_bias=rel_bias, attn_subln=attn_subln, w_attn_o=w_attn_o, conv_w=conv_w, conv_b=conv_b,
             a_log=a_log, dt_bias=dt_bias, d_skip=d_skip, ssm_norm=ssm_norm, w_ssm_o=w_ssm_o, w_out=w_out,
             norm_cross=norm_cross, norm_mem=norm_mem, w_cq=w_cq, w_ckv=w_ckv, w_co=w_co, norm_ffn=norm_ffn,
             w_pq=w_pq, sub_keys=sub_keys, expert_u=expert_u, expert_v=expert_v, norm_final=norm_final)
    w = _prepare(p)
    y_prompt = _trunk(x_prompt, mem_prompt, p, w)
    y_sample = _trunk(x_sample, mem_sample, p, w)
    return (y_prompt, y_sample)
```

```python
import functools
import math

import jax
import jax.numpy as jnp
import numpy as np
from jax import lax
from jax.experimental import pallas as pl
from jax.experimental.pallas import tpu as pltpu

F32 = jnp.float32
BF16 = jnp.bfloat16

D_MODEL = 2048
ATT_HEADS = 8
ATT_V_DIM = 128
ATT_QK_DIM = 64
N_BUCKETS = 32
MAX_DISTANCE = 128
D_INNER = 1024
SSM_HEADDIM = 64
SSM_HEADS = 16
SSM_GROUPS = 2
HEADS_PER_GROUP = 8
D_STATE = 128
D_CONV = 5
CONV_PAD = 2
CONV_DIM = D_INNER + 2 * SSM_GROUPS * D_STATE
SSM_CHUNK = 128
CROSS_HEADS = 4
CROSS_DIM = 512
PEER_HEADS = 8
N_KEYS = 128
N_EXPERTS = N_KEYS * N_KEYS
PEER_TOPK = 16
D_KEY = 128
D_KEY_HALF = 64
EPS = 1e-6

Q_COLS = 1024
V_COLS = 1024
DT_COLS = 32
LANES = 128
SUBLANES = 8
HALO = SUBLANES
ATTN_TILE = 512
ONES_ROWS = 16
ATTN_GROUP = 256
VMEM_LIMIT = 56 * 1024 * 1024


def _cparams(sem):
    return pltpu.CompilerParams(dimension_semantics=sem, vmem_limit_bytes=VMEM_LIMIT)


def _pick(n, pref):
    t = min(pref, n)
    while n % t:
        t //= 2
    return t


def _rmsnorm_kernel(x_ref, g_ref, o_ref):
    x = x_ref[...]
    y = x * lax.rsqrt(jnp.mean(x * x, axis=-1, keepdims=True) + EPS)
    o_ref[...] = (y * g_ref[...]).astype(o_ref.dtype)


def _rmsnorm_bf16(x, g):
    m, d = x.shape
    tm = _pick(m, 512)
    return pl.pallas_call(
        _rmsnorm_kernel,
        grid=(m // tm,),
        in_specs=[pl.BlockSpec((tm, d), lambda i: (i, 0)), pl.BlockSpec((1, d), lambda i: (0, 0))],
        out_specs=pl.BlockSpec((tm, d), lambda i: (i, 0)),
        out_shape=jax.ShapeDtypeStruct((m, d), BF16),
        compiler_params=_cparams(("parallel",)),
        name="rmsnorm_bf16",
    )(x, g.reshape(1, d).astype(F32))


def _mm_kernel(x_ref, w_ref, o_ref):
    o_ref[...] = jnp.dot(x_ref[...], w_ref[...], preferred_element_type=F32).astype(o_ref.dtype)


def _mm_residual_norm_kernel(x_ref, w_ref, r_ref, g_ref, o_ref, n_ref):
    y = r_ref[...] + jnp.dot(x_ref[...], w_ref[...], preferred_element_type=F32)
    o_ref[...] = y
    n_ref[...] = (y * lax.rsqrt(jnp.mean(y * y, axis=-1, keepdims=True) + EPS) * g_ref[...]).astype(n_ref.dtype)


def _mm_residual_norm(x, w, residual, gain, tm=512, name="mm_res_norm"):
    m, k = x.shape
    n = w.shape[1]
    tm = _pick(m, tm)
    row = lambda i: (i, 0)
    return pl.pallas_call(
        _mm_residual_norm_kernel,
        grid=(m // tm,),
        in_specs=[pl.BlockSpec((tm, k), row), pl.BlockSpec((k, n), lambda i: (0, 0)), pl.BlockSpec((tm, n), row),
                  pl.BlockSpec((1, n), lambda i: (0, 0))],
        out_specs=[pl.BlockSpec((tm, n), row), pl.BlockSpec((tm, n), row)],
        out_shape=[jax.ShapeDtypeStruct((m, n), F32), jax.ShapeDtypeStruct((m, n), BF16)],
        compiler_params=_cparams(("parallel",)),
        name=name,
    )(x, w, residual, gain.reshape(1, n).astype(F32))


def _mm_merge_kernel(x_ref, w_ref, t_ref, g0_ref, g1_ref, o_ref):
    acc = jnp.dot(x_ref[...], w_ref[...], preferred_element_type=F32)
    g0 = jax.nn.sigmoid(g0_ref[...])
    g1 = jax.nn.sigmoid(g1_ref[...])
    o_ref[...] = (g0 * t_ref[...] + g1 * acc).astype(o_ref.dtype)


def _mm(x, w, out_dtype, *, merge=None, tm=1024, tn=512, name="mm"):
    m, k = x.shape
    n = w.shape[1]
    tm = _pick(m, tm)
    tn = _pick(n, tn)
    in_specs = [pl.BlockSpec((tm, k), lambda i, j: (i, 0)), pl.BlockSpec((k, tn), lambda i, j: (0, j))]
    args = [x, w]
    kern = _mm_kernel
    if merge is not None:
        kern = _mm_merge_kernel
        t_att, gates = merge
        nb = n // tn
        in_specs += [
            pl.BlockSpec((tm, tn), lambda i, j: (i, j)),
            pl.BlockSpec((tm, tn), lambda i, j: (i, j)),
            pl.BlockSpec((tm, tn), lambda i, j: (i, j + nb)),
        ]
        args += [t_att, gates, gates]
    return pl.pallas_call(
        kern,
        grid=(m // tm, n // tn),
        in_specs=in_specs,
        out_specs=pl.BlockSpec((tm, tn), lambda i, j: (i, j)),
        out_shape=jax.ShapeDtypeStruct((m, n), out_dtype),
        compiler_params=_cparams(("parallel", "parallel")),
        name=name,
    )(*args)


def _t5_bucket(rel):
    half = N_BUCKETS // 2
    exact = half // 2
    n = jnp.abs(rel)
    far = exact + (
        jnp.log(jnp.maximum(n, 1).astype(F32) / exact) / math.log(MAX_DISTANCE / exact) * (half - exact)
    ).astype(jnp.int32)
    far = jnp.minimum(far, half - 1)
    return jnp.where(rel > 0, half, 0) + jnp.where(n < exact, n, far)


def _attn_kernel(lam_ref, far_ref, q_ref, k_ref, vt_ref, bias_ref, g_ref, o_ref, q2_scr, sa_scr, sb_scr, m_scr, acc_scr, *, t,
                 nk, out_scale):
    h = pl.program_id(1)
    qi = pl.program_id(2)
    qt = (q_ref[...].astype(F32) * (ATT_QK_DIM ** -0.5)).T
    row = lax.broadcasted_iota(jnp.int32, qt.shape, 0)
    q2_scr[:, 0:t] = jnp.where(row < ATT_QK_DIM, qt, 0.0).astype(BF16)
    q2_scr[:, t:2 * t] = jnp.where(row >= ATT_QK_DIM, qt, 0.0).astype(BF16)
    m_scr[...] = jnp.full(m_scr.shape, -jnp.inf, F32)
    acc_scr[...] = jnp.zeros(acc_scr.shape, F32)

    gw = ATTN_GROUP
    ngroups = 2 * t // gw

    def scores(kc, g):
        start = pl.multiple_of(kc * t, t)
        return jnp.dot(k_ref[pl.ds(start, t), :], q2_scr[:, g * gw:(g + 1) * gw],
                       preferred_element_type=F32)

    def update(g, st, vt, const, tile_idx):
        cols = slice(g * gw, (g + 1) * gw)
        if tile_idx is not None:
            b0 = (g * gw) % t
            st = st + bias_ref[tile_idx, :, b0:b0 + gw]
        m_cur = jnp.max(st, axis=0, keepdims=True)
        if const is not None:
            m_cur = m_cur + const
        m_old = m_scr[:, cols]
        m_new = jnp.maximum(m_old, m_cur)
        alpha = jnp.exp(m_old - m_new)
        shift = -m_new if const is None else const - m_new
        pt = jnp.exp(st + shift)
        acc_scr[:, cols] = alpha * acc_scr[:, cols] + jnp.dot(vt, pt.astype(BF16), preferred_element_type=F32)
        m_scr[:, cols] = m_new

    lo_end = jnp.maximum(qi - 1, 0)
    near_end = jnp.minimum(qi + 2, nk)
    three = (near_end - lo_end) == 3
    t_start = lo_end - jnp.where(three & (qi + 2 > nk - 1), 1, 0)
    n_tile = jnp.where(three, 4, 2)
    n_const = nk - n_tile

    def tile_chunk(j):
        kc = jnp.minimum(t_start + j, nk - 1)
        rel = kc - qi
        return kc, jnp.where(rel < -1, 3, jnp.where(rel > 1, 4, rel + 1))

    def const_chunk(f):
        f = jnp.minimum(f, n_const - 1)
        lo = f < t_start
        return jnp.where(lo, f, f + n_tile), jnp.where(lo, far_ref[h, 0], far_ref[h, 1])

    def pipeline(n_chunks, chunk_of, use_tile):
        @pl.when(n_chunks >= 2)
        def _():
            k0, _ = chunk_of(0)
            for g in range(ngroups):
                sa_scr[:, g * gw:(g + 1) * gw] = scores(k0, g)

        def pair_body(j, carry):
            (ka, xa), (kb, xb), (kn, _) = chunk_of(2 * j), chunk_of(2 * j + 1), chunk_of(2 * j + 2)
            for src, dst, k_cur, k_nxt, x in ((sa_scr, sb_scr, ka, kb, xa), (sb_scr, sa_scr, kb, kn, xb)):
                vt = vt_ref[k_cur]
                for g in range(ngroups):
                    cols = slice(g * gw, (g + 1) * gw)
                    dst[:, cols] = scores(k_nxt, g)
                    if use_tile:
                        update(g, src[:, cols], vt, None, x)
                    else:
                        update(g, src[:, cols], vt, x, None)
            return carry

        lax.fori_loop(0, n_chunks // 2, pair_body, 0)

    pipeline(n_tile, tile_chunk, True)
    pipeline(n_const, const_chunk, False)

    lam = lam_ref[0]
    dv = ATT_V_DIM
    ot = (acc_scr[0:dv, 0:t] / acc_scr[dv:dv + 1, 0:t]
          - lam * (acc_scr[0:dv, t:2 * t] / acc_scr[dv:dv + 1, t:2 * t]))
    o = ot.T
    y = o * lax.rsqrt(jnp.mean(o * o, axis=-1, keepdims=True) + EPS)
    o_ref[...] = ((y * g_ref[...]) * out_scale).astype(o_ref.dtype)


def _diff_attention(qkv, b, s, rel_bias, lam, subln, out_scale, t):
    nk = s // t
    hq = ATT_HEADS
    r = jnp.arange(t, dtype=jnp.int32)
    rel = (jnp.arange(3, dtype=jnp.int32)[:, None, None] - 1) * t + r[None, :, None] - r[None, None, :]
    bucket = _t5_bucket(rel)[None]
    rb32 = rel_bias.astype(F32)
    bias_tiles = jnp.zeros((hq, 3, t, t), F32)
    for n in range(N_BUCKETS):
        bias_tiles = jnp.where(bucket == n, rb32[n][:, None, None, None], bias_tiles)
    half = N_BUCKETS // 2
    far = jnp.stack([rel_bias[half - 1], rel_bias[N_BUCKETS - 1]], axis=-1).astype(F32)
    bias_tiles = jnp.concatenate([bias_tiles, jnp.broadcast_to(far[:, :, None, None], (hq, 2, t, t))], axis=1)
    assert nk % 2 == 0 and t % ATTN_GROUP == 0
    vt = qkv[:, 2 * Q_COLS:].reshape(b, nk, t, hq, ATT_V_DIM).transpose(0, 3, 1, 4, 2)
    vt = jnp.concatenate([vt, jnp.ones((b, hq, nk, ONES_ROWS, t), BF16)], axis=3)
    kern = functools.partial(_attn_kernel, t=t, nk=nk, out_scale=out_scale)
    return pl.pallas_call(
        kern,
        grid=(b, hq, nk),
        in_specs=[
            pl.BlockSpec(memory_space=pltpu.SMEM),
            pl.BlockSpec(memory_space=pltpu.SMEM),
            pl.BlockSpec((t, LANES), lambda bi, h, qi: (bi * nk + qi, h)),
            pl.BlockSpec((s, LANES), lambda bi, h, qi: (bi, hq + h)),
            pl.BlockSpec((None, None, nk, ATT_V_DIM + ONES_ROWS, t), lambda bi, h, qi: (bi, h, 0, 0, 0)),
            pl.BlockSpec((None, 5, t, t), lambda bi, h, qi: (h, 0, 0, 0)),
            pl.BlockSpec((1, LANES), lambda bi, h, qi: (0, 0)),
        ],
        out_specs=pl.BlockSpec((t, LANES), lambda bi, h, qi: (bi * nk + qi, h)),
        out_shape=jax.ShapeDtypeStruct((b * s, V_COLS), BF16),
        scratch_shapes=[
            pltpu.VMEM((LANES, 2 * t), BF16),
            pltpu.VMEM((t, 2 * t), F32),
            pltpu.VMEM((t, 2 * t), F32),
            pltpu.VMEM((1, 2 * t), F32),
            pltpu.VMEM((ATT_V_DIM + ONES_ROWS, 2 * t), F32),
        ],
        compiler_params=_cparams(("parallel", "parallel", "parallel")),
        name="diff_attention",
    )(lam.reshape(1).astype(F32), far, qkv, qkv, vt, bias_tiles, subln.reshape(1, LANES).astype(F32))


def _conv_kernel(prev_ref, cur_ref, next_ref, w_ref, b_ref, o_ref, ext_scr, *, tb, ns):
    si = pl.program_id(1)
    ext_scr[0:HALO, :] = jnp.where(si > 0, prev_ref[...], 0.0)
    ext_scr[HALO:HALO + tb, :] = cur_ref[...]
    ext_scr[HALO + tb:2 * HALO + tb, :] = jnp.where(si < ns - 1, next_ref[...], 0.0)
    y = b_ref[...]
    for j in range(D_CONV):
        off = HALO - CONV_PAD + j
        y = y + w_ref[j:j + 1, :] * ext_scr[off:off + tb, :]
    o_ref[...] = y * jax.nn.sigmoid(y)


def _conv_silu(xbc, b, s, conv_w, conv_b, tb=1024, tc=CONV_DIM // 2):
    tb = _pick(s, tb)
    ns = s // tb
    hb = tb // HALO
    nc = CONV_DIM // tc
    kern = functools.partial(_conv_kernel, tb=tb, ns=ns)
    return pl.pallas_call(
        kern,
        grid=(b, ns, nc),
        in_specs=[
            pl.BlockSpec((HALO, tc), lambda bi, si, ci: (jnp.maximum((bi * ns + si) * hb - 1, 0), ci)),
            pl.BlockSpec((tb, tc), lambda bi, si, ci: (bi * ns + si, ci)),
            pl.BlockSpec((HALO, tc), lambda bi, si, ci: (jnp.minimum((bi * ns + si + 1) * hb, b * ns * hb - 1), ci)),
            pl.BlockSpec((D_CONV, tc), lambda bi, si, ci: (0, ci)),
            pl.BlockSpec((1, tc), lambda bi, si, ci: (0, ci)),
        ],
        out_specs=pl.BlockSpec((tb, tc), lambda bi, si, ci: (bi * ns + si, ci)),
        out_shape=jax.ShapeDtypeStruct(xbc.shape, F32),
        scratch_shapes=[pltpu.VMEM((tb + 2 * HALO, tc), F32)],
        compiler_params=_cparams(("parallel", "parallel", "parallel")),
        name="conv_silu",
    )(xbc, xbc, xbc, conv_w.astype(F32), conv_b.reshape(1, CONV_DIM).astype(F32))


def _softplus(x):
    return jnp.maximum(x, 0.0) + jnp.log1p(jnp.exp(-jnp.abs(x)))


def _ssd_kernel(*refs, reverse):
    if reverse:
        x_ref, dt_ref, dtb_ref, a_ref, yf_ref, z_ref, dsk_ref, gn_ref, o_ref, st_scr = refs
    else:
        x_ref, dt_ref, dtb_ref, a_ref, o_ref, st_scr = refs
    L = SSM_CHUNK
    d = 1 if reverse else 0

    @pl.when(pl.program_id(1) == 0)
    def _():
        st_scr[...] = jnp.zeros(st_scr.shape, F32)

    dt = _softplus(dt_ref[...] + dtb_ref[...])
    a = dt * a_ref[...]
    row = lax.broadcasted_iota(jnp.int32, (L, L), 0)
    col = lax.broadcasted_iota(jnp.int32, (L, L), 1)
    tril = (row >= col).astype(F32)
    acs = jnp.dot(tril, a, preferred_element_type=F32, precision=lax.Precision.HIGHEST)
    ecs = acs - a if reverse else acs
    ecs_t = ecs.T
    keep = (col >= row) if reverse else (row >= col)
    total = jnp.broadcast_to(acs[L - 1:L, :], (SUBLANES, LANES))
    if reverse:
        out_decay = jnp.exp(total[0:1, :] - ecs)
        in_decay = jnp.exp(ecs)
    else:
        out_decay = jnp.exp(ecs)
        in_decay = jnp.exp(total[0:1, :] - ecs)

    ch_head = lax.broadcasted_iota(jnp.int32, (LANES, D_INNER), 1) // SSM_HEADDIM + d * SSM_HEADS
    spread = jnp.where(lax.broadcasted_iota(jnp.int32, (LANES, D_INNER), 0) == ch_head, 1.0, 0.0).astype(BF16)

    def per_channel(v):
        hi = v.astype(BF16)
        lo = (v - hi.astype(F32)).astype(BF16)
        return (jnp.dot(hi, spread, preferred_element_type=F32) + jnp.dot(lo, spread, preferred_element_type=F32))

    xdt = x_ref[:, 0:D_INNER] * per_channel(dt)
    xdt16 = xdt.astype(BF16)
    xin16 = (xdt * per_channel(in_decay)).astype(BF16)
    out_ch = per_channel(out_decay)
    tot_ch = per_channel(jnp.exp(total))[0:1, :]
    first = lax.broadcasted_iota(jnp.int32, (L, LANES), 1) < SSM_HEADDIM

    ys = []
    for g in range(SSM_GROUPS):
        bg = x_ref[:, D_INNER + g * D_STATE:D_INNER + (g + 1) * D_STATE]
        cg = x_ref[:, D_INNER + SSM_GROUPS * D_STATE + g * D_STATE:D_INNER + SSM_GROUPS * D_STATE + (g + 1) * D_STATE]
        cg16 = cg.astype(BF16)
        cb = lax.dot_general(cg16, bg.astype(BF16), (((1,), (1,)), ((), ())), preferred_element_type=F32)
        bgt16 = bg.T.astype(BF16)
        for pr in range(HEADS_PER_GROUP // 2):
            pair = g * (HEADS_PER_GROUP // 2) + pr
            lanes = slice(pair * LANES, (pair + 1) * LANES)
            xp = xdt16[:, lanes]
            halves = (jnp.where(first, xp, jnp.zeros_like(xp)), jnp.where(first, jnp.zeros_like(xp), xp))
            y = None
            for k in range(2):
                c = d * SSM_HEADS + 2 * pair + k
                e_col = ecs[:, c:c + 1]
                e_row = ecs_t[c:c + 1, :]
                diff = e_row - e_col if reverse else e_col - e_row
                seg = jnp.exp(jnp.where(keep, diff, -jnp.inf))
                yk = jnp.dot((seg * cb).astype(BF16), halves[k], preferred_element_type=F32)
                y = yk if y is None else y + yk
            st = st_scr[pair]
            y = y + jnp.dot(cg16, st.astype(BF16), preferred_element_type=F32) * out_ch[:, lanes]
            st_scr[pair] = st * tot_ch[:, lanes] + jnp.dot(bgt16, xin16[:, lanes], preferred_element_type=F32)
            ys.append(y)
    y = jnp.concatenate(ys, axis=-1)
    if not reverse:
        o_ref[...] = y
        return
    xs = x_ref[:, 0:D_INNER]
    y = y + yf_ref[...] + dsk_ref[...] * xs
    z = z_ref[...]
    y = y * (z * jax.nn.sigmoid(z))
    gw = D_INNER // SSM_GROUPS
    outs = []
    for g in range(SSM_GROUPS):
        yg = y[:, g * gw:(g + 1) * gw]
        yn = yg * lax.rsqrt(jnp.mean(yg * yg, axis=-1, keepdims=True) + EPS)
        outs.append(yn * gn_ref[:, g * gw:(g + 1) * gw])
    o_ref[...] = jnp.concatenate(outs, axis=-1).astype(o_ref.dtype)


def _ssd(xbc_act, dt_raw, z, b, s, dt_bias, a_log, d_skip, ssm_norm):
    L = SSM_CHUNK
    nc = s // L
    pad = LANES - DT_COLS
    dtb = jnp.pad(dt_bias.astype(F32).reshape(1, DT_COLS), ((0, 0), (0, pad)))
    a_neg = jnp.pad(-jnp.exp(a_log.astype(F32)).reshape(1, DT_COLS), ((0, 0), (0, pad)))
    dsk = d_skip.astype(F32)
    dsk = jnp.repeat(dsk[0] + dsk[1], SSM_HEADDIM).reshape(1, D_INNER)
    st_shape = pltpu.VMEM((SSM_HEADS // 2, D_STATE, 2 * SSM_HEADDIM), F32)
    small = lambda w: pl.BlockSpec((1, w), lambda bi, ci: (0, 0))

    fwd_map = lambda bi, ci: (bi * nc + ci, 0)
    y_f = pl.pallas_call(
        functools.partial(_ssd_kernel, reverse=False),
        grid=(b, nc),
        in_specs=[pl.BlockSpec((L, CONV_DIM), fwd_map), pl.BlockSpec((L, LANES), fwd_map), small(LANES), small(LANES)],
        out_specs=pl.BlockSpec((L, D_INNER), fwd_map),
        out_shape=jax.ShapeDtypeStruct((b * s, D_INNER), F32),
        scratch_shapes=[st_shape],
        compiler_params=_cparams(("parallel", "arbitrary")),
        name="ssd_fwd",
    )(xbc_act, dt_raw, dtb, a_neg)

    bwd_map = lambda bi, ci: (bi * nc + (nc - 1 - ci), 0)
    return pl.pallas_call(
        functools.partial(_ssd_kernel, reverse=True),
        grid=(b, nc),
        in_specs=[
            pl.BlockSpec((L, CONV_DIM), bwd_map),
            pl.BlockSpec((L, LANES), bwd_map),
            small(LANES),
            small(LANES),
            pl.BlockSpec((L, D_INNER), bwd_map),
            pl.BlockSpec((L, D_INNER), bwd_map),
            small(D_INNER),
            small(D_INNER),
        ],
        out_specs=pl.BlockSpec((L, D_INNER), bwd_map),
        out_shape=jax.ShapeDtypeStruct((b * s, D_INNER), BF16),
        scratch_shapes=[st_shape],
        compiler_params=_cparams(("parallel", "arbitrary")),
        name="ssd_bwd",
    )(xbc_act, dt_raw, dtb, a_neg, y_f, z, dsk, ssm_norm.astype(F32).reshape(1, D_INNER))


def _cross_kernel(q_ref, k_ref, v_ref, o_ref):
    outs = []
    for h in range(CROSS_HEADS):
        sl = slice(h * CROSS_DIM, (h + 1) * CROSS_DIM)
        s = lax.dot_general(q_ref[:, sl], k_ref[:, sl], (((1,), (1,)), ((), ())), preferred_element_type=F32)
        s = s * (CROSS_DIM ** -0.5)
        p = jnp.exp(s - jnp.max(s, axis=-1, keepdims=True))
        p = p / jnp.sum(p, axis=-1, keepdims=True)
        outs.append(jnp.dot(p.astype(BF16), v_ref[:, sl], preferred_element_type=F32))
    o_ref[...] = jnp.concatenate(outs, axis=-1).astype(o_ref.dtype)


def _cross_core(q, kv, b, s, n_mem, tq=512):
    tq = _pick(s, tq)
    nq = s // tq
    return pl.pallas_call(
        _cross_kernel,
        grid=(b, nq),
        in_specs=[
            pl.BlockSpec((tq, D_MODEL), lambda bi, qi: (bi * nq + qi, 0)),
            pl.BlockSpec((n_mem, D_MODEL), lambda bi, qi: (bi, 0)),
            pl.BlockSpec((n_mem, D_MODEL), lambda bi, qi: (bi, 1)),
        ],
        out_specs=pl.BlockSpec((tq, D_MODEL), lambda bi, qi: (bi * nq + qi, 0)),
        out_shape=jax.ShapeDtypeStruct((b * s, D_MODEL), BF16),
        compiler_params=_cparams(("parallel", "parallel")),
        name="cross_attention",
    )(q, kv, kv)


def _top16(s, vals_scr, idx_scr, payload=None):
    n = s.shape[0]
    pos = lax.broadcasted_iota(jnp.int32, s.shape, 0).astype(F32)
    for k in range(PEER_TOPK):
        m = jnp.max(s, axis=0, keepdims=True)
        first = jnp.min(jnp.where(s == m, pos, float(n)), axis=0, keepdims=True)
        sel = pos == first
        vals_scr[k:k + 1, :] = m
        if payload is None:
            idx_scr[k:k + 1, :] = first
        else:
            idx_scr[k:k + 1, :] = jnp.max(jnp.where(sel, payload, -1.0), axis=0, keepdims=True)
        s = jnp.where(sel, -jnp.inf, s)


def _topk_kernel(q_ref, keys_ref, idx_ref, gate_ref, v1_scr, i1_scr, v2_scr, i2_scr, tv_scr, ti_scr):
    for h in range(PEER_HEADS):
        qh = q_ref[:, h * D_KEY:(h + 1) * D_KEY]
        sc = lax.dot_general(keys_ref[h], qh, (((1,), (1,)), ((), ())), preferred_element_type=F32)
        _top16(sc[0:N_KEYS], v1_scr, i1_scr)
        _top16(sc[N_KEYS:2 * N_KEYS], v2_scr, i2_scr)
        hs = SUBLANES
        pairs = [(slice(0, 1), slice(0, hs)), (slice(0, 1), slice(hs, 2 * hs))]
        pairs += [(slice(i, i + 1), slice(0, hs)) for i in range(1, hs)]
        pairs += [(slice(hs, 2 * hs), slice(0, 1))]
        cand = jnp.concatenate([v1_scr[a, :] + v2_scr[b, :] for a, b in pairs], axis=0)
        eidx = jnp.concatenate([i1_scr[a, :] * N_KEYS + i2_scr[b, :] for a, b in pairs], axis=0)
        _top16(cand, tv_scr, ti_scr, payload=eidx)
        top = tv_scr[...]
        e = jnp.exp(top - top[0:1, :])
        gate_ref[h * PEER_TOPK:(h + 1) * PEER_TOPK, :] = e / jnp.sum(e, axis=0, keepdims=True)
        idx_ref[h * PEER_TOPK:(h + 1) * PEER_TOPK, :] = ti_scr[...]


def _peer_topk(pq, keys_bd, tb=128):
    tt = pq.shape[0]
    nhk = PEER_HEADS * PEER_TOPK
    sc16 = lambda: pltpu.VMEM((PEER_TOPK, tb), F32)
    return pl.pallas_call(
        _topk_kernel,
        grid=(tt // tb,),
        in_specs=[
            pl.BlockSpec((tb, PEER_HEADS * D_KEY), lambda i: (i, 0)),
            pl.BlockSpec((PEER_HEADS, 2 * N_KEYS, D_KEY), lambda i: (0, 0, 0)),
        ],
        out_specs=[pl.BlockSpec((nhk, tb), lambda i: (0, i)), pl.BlockSpec((nhk, tb), lambda i: (0, i))],
        out_shape=[jax.ShapeDtypeStruct((nhk, tt), F32), jax.ShapeDtypeStruct((nhk, tt), F32)],
        scratch_shapes=[sc16() for _ in range(6)],
        compiler_params=_cparams(("parallel",)),
        name="peer_topk",
    )(pq, keys_bd)


GATE_GROUP = 16
GATE_PITCH = N_KEYS + 8


def _gate_kernel(idx_ref, gate_ref, o_ref, idx_scr, gate_scr, ga_scr, gb_scr, *, tg):
    idx_scr[...] = idx_ref[...].T
    gate_scr[...] = gate_ref[...].T
    pos = lax.broadcasted_iota(jnp.int32, (N_KEYS, PEER_HEADS * PEER_TOPK), 0)
    half = GATE_GROUP // 2
    ngroups = tg // GATE_GROUP

    def tiles(gi, dst):
        base = pl.multiple_of(gi * GATE_GROUP, GATE_GROUP)
        for u in range(GATE_GROUP):
            e = idx_scr[pl.ds(base + u, 1), :].astype(jnp.int32)
            g = gate_scr[pl.ds(base + u, 1), :]
            a_t = jnp.where(pos == (e >> 7), 1.0, 0.0).astype(BF16)
            b_t = jnp.where(pos == (e & (N_KEYS - 1)), g, 0.0).astype(BF16)
            dst[u * GATE_PITCH:u * GATE_PITCH + N_KEYS, :] = lax.dot_general(
                a_t, b_t, (((1,), (1,)), ((), ())), preferred_element_type=F32)

    def relayout(gi, src):
        base = pl.multiple_of(gi * GATE_GROUP, GATE_GROUP)
        for e1 in range(N_KEYS):
            lo = src[pl.ds(e1, half, stride=GATE_PITCH), :]
            hi = src[pl.ds(half * GATE_PITCH + e1, half, stride=GATE_PITCH), :]
            o_ref[e1, pl.ds(base, GATE_GROUP), :] = jnp.concatenate([lo, hi], axis=0).astype(o_ref.dtype)

    tiles(0, ga_scr)

    def body(j, carry):
        tiles(2 * j + 1, gb_scr)
        relayout(2 * j, ga_scr)
        tiles(jnp.minimum(2 * j + 2, ngroups - 1), ga_scr)
        relayout(2 * j + 1, gb_scr)
        return carry

    lax.fori_loop(0, ngroups // 2, body, 0)


def _peer_gates(idx, gate, tg=256):
    nhk, tt = idx.shape
    return pl.pallas_call(
        functools.partial(_gate_kernel, tg=tg),
        grid=(tt // tg,),
        in_specs=[pl.BlockSpec((nhk, tg), lambda i: (0, i)), pl.BlockSpec((nhk, tg), lambda i: (0, i))],
        out_specs=pl.BlockSpec((N_KEYS, tg, N_KEYS), lambda i: (0, i, 0)),
        out_shape=jax.ShapeDtypeStruct((N_KEYS, tt, N_KEYS), BF16),
        scratch_shapes=[
            pltpu.VMEM((tg, nhk), F32),
            pltpu.VMEM((tg, nhk), F32),
            pltpu.VMEM((GATE_GROUP * GATE_PITCH, N_KEYS), F32),
            pltpu.VMEM((GATE_GROUP * GATE_PITCH, N_KEYS), F32),
        ],
        compiler_params=_cparams(("parallel",)),
        name="peer_gates",
    )(idx, gate)


def _gelu_tanh(x):
    return x * (0.5 * (1.0 + jnp.tanh(math.sqrt(2.0 / math.pi) * (x + 0.044715 * (x * x * x)))))


def _ffn_kernel(h_ref, u_ref, v_ref, g_ref, x_ref, nf_ref, o_ref):
    j = pl.program_id(1)

    @pl.when(j == 0)
    def _():
        o_ref[...] = jnp.zeros(o_ref.shape, F32)

    s = jnp.dot(h_ref[...], u_ref[...], preferred_element_type=F32)
    coeff = jnp.concatenate(
        [g_ref[k].astype(F32) * _gelu_tanh(s[:, k * LANES:(k + 1) * LANES]) for k in range(g_ref.shape[0])], axis=-1
    ).astype(BF16)
    o_ref[...] += jnp.dot(coeff, v_ref[...], preferred_element_type=F32)

    @pl.when(j == pl.num_programs(1) - 1)
    def _():
        x = x_ref[...] + o_ref[...]
        y = x * lax.rsqrt(jnp.mean(x * x, axis=-1, keepdims=True) + EPS)
        o_ref[...] = y * nf_ref[...]


def _peer_ffn(hp, ut16, v16, gmat, x_res, norm_final, tm=512, te=1024):
    tt = hp.shape[0]
    tm = _pick(tt, tm)
    return pl.pallas_call(
        _ffn_kernel,
        grid=(tt // tm, N_EXPERTS // te),
        in_specs=[
            pl.BlockSpec((tm, D_MODEL), lambda i, j: (i, 0)),
            pl.BlockSpec((D_MODEL, te), lambda i, j: (0, j)),
            pl.BlockSpec((te, D_MODEL), lambda i, j: (j, 0)),
            pl.BlockSpec((te // LANES, tm, LANES), lambda i, j: (j, i, 0)),
            pl.BlockSpec((tm, D_MODEL), lambda i, j: (i, 0)),
            pl.BlockSpec((1, D_MODEL), lambda i, j: (0, 0)),
        ],
        out_specs=pl.BlockSpec((tm, D_MODEL), lambda i, j: (i, 0)),
        out_shape=jax.ShapeDtypeStruct((tt, D_MODEL), F32),
        compiler_params=_cparams(("parallel", "arbitrary")),
        name="peer_ffn",
    )(hp, ut16, v16, gmat, x_res, norm_final.reshape(1, D_MODEL).astype(F32))


def _prepare(p):
    w_in = p["w_in"][0]
    c = np.cumsum([0, 3 * Q_COLS, D_INNER, CONV_DIM, DT_COLS, 2 * D_MODEL])
    w = {}
    w["qkv"] = w_in[:, c[0]:c[1]].astype(BF16)
    w["z"] = w_in[:, c[1]:c[2]].astype(BF16)
    w["xbc"] = w_in[:, c[2]:c[3]].astype(BF16)
    w["dt"] = jnp.pad(w_in[:, c[3]:c[4]], ((0, 0), (0, LANES - DT_COLS))).astype(BF16)
    w["gates"] = w_in[:, c[4]:c[5]].astype(BF16)
    for name in ("w_attn_o", "w_ssm_o", "w_out", "w_cq", "w_ckv", "w_co", "w_pq", "expert_v"):
        w[name] = p[name][0].astype(BF16)
    w["expert_ut"] = p["expert_u"][0].astype(BF16).T
    sk = p["sub_keys"][0].astype(BF16)
    zero = jnp.zeros_like(sk[:, 0])
    w["keys_bd"] = jnp.concatenate(
        [jnp.concatenate([sk[:, 0], zero], axis=-1), jnp.concatenate([zero, sk[:, 1]], axis=-1)], axis=1
    )
    lam_init = 0.8 - 0.6 * math.exp(-0.3 * 0)
    w["lam"] = (
        jnp.exp(jnp.sum(p["lam_q1"][0].astype(F32) * p["lam_k1"][0].astype(F32)))
        - jnp.exp(jnp.sum(p["lam_q2"][0].astype(F32) * p["lam_k2"][0].astype(F32)))
        + lam_init
    )
    w["lam_init"] = lam_init
    return w


def _trunk(x, mem, p, w):
    b, s, d = x.shape
    n_mem = mem.shape[1]
    xf = x.reshape(b * s, d)

    h1 = _rmsnorm_bf16(xf, p["norm_mix"][0])
    qkv = _mm(h1, w["qkv"], BF16, name="mm_qkv")
    z = _mm(h1, w["z"], F32, name="mm_z")
    xbc = _mm(h1, w["xbc"], F32, tn=512, name="mm_xbc")
    dt_raw = _mm(h1, w["dt"], F32, name="mm_dt")
    gates = _mm(h1, w["gates"], F32, name="mm_gates")

    o_att = _diff_attention(qkv, b, s, p["rel_bias"], w["lam"], p["attn_subln"][0], 1.0 - w["lam_init"],
                            _pick(s, ATTN_TILE))
    xbc_act = _conv_silu(xbc, b, s, p["conv_w"][0], p["conv_b"][0])
    y_ssm = _ssd(xbc_act, dt_raw, z, b, s, p["dt_bias"][0], p["a_log"][0], p["d_skip"][0], p["ssm_norm"][0])

    t_att = _mm(o_att, w["w_attn_o"], F32, name="mm_attn_o")
    merged = _mm(y_ssm, w["w_ssm_o"], BF16, merge=(t_att, gates), name="mm_ssm_o_merge")
    x1, hq = _mm_residual_norm(merged, w["w_out"], xf, p["norm_cross"][0], name="mm_out")

    q = _mm(hq, w["w_cq"], BF16, name="mm_cq")
    mn = _rmsnorm_bf16(mem.reshape(b * n_mem, d), p["norm_mem"][0])
    kv = _mm(mn, w["w_ckv"], BF16, name="mm_ckv")
    oc = _cross_core(q, kv, b, s, n_mem)
    x2, hp = _mm_residual_norm(oc, w["w_co"], x1, p["norm_ffn"][0], name="mm_co")

    pq = _mm(hp, w["w_pq"], BF16, name="mm_pq")
    idx, gate = _peer_topk(pq, w["keys_bd"])
    gmat = _peer_gates(idx, gate)
    y = _peer_ffn(hp, w["expert_ut"], w["expert_v"], gmat, x2, p["norm_final"])
    return y.reshape(b, s, d)


def kernel(x_prompt, x_sample, mem_prompt, mem_sample, norm_mix, w_in, lam_q1, lam_k1, lam_q2, lam_k2, rel_bias, attn_subln, w_attn_o, conv_w, conv_b, a_log, dt_bias, d_skip, ssm_norm, w_ssm_o, w_out, norm_cross, norm_mem, w_cq, w_ckv, w_co, norm_ffn, w_pq, sub_keys, expert_u, expert_v, norm_final):
    p = dict(norm_mix=norm_mix, w_in=w_in, lam_q1=lam_q1, lam_k1=lam_k1, lam_q2=lam_q2, lam_k2=lam_k2,
             rel_bias=rel_bias, attn_subln=attn_subln, w_attn_o=w_attn_o, conv_w=conv_w, conv_b=conv_b,
             a_log=a_log, dt_bias=dt_bias, d_skip=d_skip, ssm_norm=ssm_norm, w_ssm_o=w_ssm_o, w_out=w_out,
             norm_cross=norm_cross, norm_mem=norm_mem, w_cq=w_cq, w_ckv=w_ckv, w_co=w_co, norm_ffn=norm_ffn,
             w_pq=w_pq, sub_keys=sub_keys, expert_u=expert_u, expert_v=expert_v, norm_final=norm_final)
    w = _prepare(p)
    y_prompt = _trunk(x_prompt, mem_prompt, p, w)
    y_sample = _trunk(x_sample, mem_sample, p, w)
    return (y_prompt, y_sample)
```

```python
import functools
import math

import jax
import jax.numpy as jnp
import numpy as np
from jax import lax
from jax.experimental import pallas as pl
from jax.experimental.pallas import tpu as pltpu

F32 = jnp.float32
BF16 = jnp.bfloat16

D_MODEL = 2048
ATT_HEADS = 8
ATT_V_DIM = 128
ATT_QK_DIM = 64
N_BUCKETS = 32
MAX_DISTANCE = 128
D_INNER = 1024
SSM_HEADDIM = 64
SSM_HEADS = 16
SSM_GROUPS = 2
HEADS_PER_GROUP = 8
D_STATE = 128
D_CONV = 5
CONV_PAD = 2
CONV_DIM = D_INNER + 2 * SSM_GROUPS * D_STATE
SSM_CHUNK = 128
CROSS_HEADS = 4
CROSS_DIM = 512
PEER_HEADS = 8
N_KEYS = 128
N_EXPERTS = N_KEYS * N_KEYS
PEER_TOPK = 16
D_KEY = 128
D_KEY_HALF = 64
EPS = 1e-6

Q_COLS = 1024
V_COLS = 1024
DT_COLS = 32
LANES = 128
SUBLANES = 8
HALO = SUBLANES
ATTN_TILE = 512
ONES_ROWS = 16
ATTN_GROUP = 512
VMEM_LIMIT = 56 * 1024 * 1024


def _cparams(sem):
    return pltpu.CompilerParams(dimension_semantics=sem, vmem_limit_bytes=VMEM_LIMIT)


def _pick(n, pref):
    t = min(pref, n)
    while n % t:
        t //= 2
    return t


def _rmsnorm_kernel(x_ref, g_ref, o_ref):
    x = x_ref[...]
    y = x * lax.rsqrt(jnp.mean(x * x, axis=-1, keepdims=True) + EPS)
    o_ref[...] = (y * g_ref[...]).astype(o_ref.dtype)


def _rmsnorm_bf16(x, g):
    m, d = x.shape
    tm = _pick(m, 512)
    return pl.pallas_call(
        _rmsnorm_kernel,
        grid=(m // tm,),
        in_specs=[pl.BlockSpec((tm, d), lambda i: (i, 0)), pl.BlockSpec((1, d), lambda i: (0, 0))],
        out_specs=pl.BlockSpec((tm, d), lambda i: (i, 0)),
        out_shape=jax.ShapeDtypeStruct((m, d), BF16),
        compiler_params=_cparams(("parallel",)),
        name="rmsnorm_bf16",
    )(x, g.reshape(1, d).astype(F32))


def _mm_kernel(x_ref, w_ref, o_ref):
    o_ref[...] = jnp.dot(x_ref[...], w_ref[...], preferred_element_type=F32).astype(o_ref.dtype)


def _mm_residual_norm_kernel(x_ref, w_ref, r_ref, g_ref, o_ref, n_ref):
    y = r_ref[...] + jnp.dot(x_ref[...], w_ref[...], preferred_element_type=F32)
    o_ref[...] = y
    n_ref[...] = (y * lax.rsqrt(jnp.mean(y * y, axis=-1, keepdims=True) + EPS) * g_ref[...]).astype(n_ref.dtype)


def _mm_residual_norm(x, w, residual, gain, tm=512, name="mm_res_norm"):
    m, k = x.shape
    n = w.shape[1]
    tm = _pick(m, tm)
    row = lambda i: (i, 0)
    return pl.pallas_call(
        _mm_residual_norm_kernel,
        grid=(m // tm,),
        in_specs=[pl.BlockSpec((tm, k), row), pl.BlockSpec((k, n), lambda i: (0, 0)), pl.BlockSpec((tm, n), row),
                  pl.BlockSpec((1, n), lambda i: (0, 0))],
        out_specs=[pl.BlockSpec((tm, n), row), pl.BlockSpec((tm, n), row)],
        out_shape=[jax.ShapeDtypeStruct((m, n), F32), jax.ShapeDtypeStruct((m, n), BF16)],
        compiler_params=_cparams(("parallel",)),
        name=name,
    )(x, w, residual, gain.reshape(1, n).astype(F32))


def _mm_merge_kernel(x_ref, w_ref, t_ref, g0_ref, g1_ref, o_ref):
    acc = jnp.dot(x_ref[...], w_ref[...], preferred_element_type=F32)
    g0 = jax.nn.sigmoid(g0_ref[...])
    g1 = jax.nn.sigmoid(g1_ref[...])
    o_ref[...] = (g0 * t_ref[...] + g1 * acc).astype(o_ref.dtype)


def _mm(x, w, out_dtype, *, merge=None, tm=1024, tn=512, name="mm"):
    m, k = x.shape
    n = w.shape[1]
    tm = _pick(m, tm)
    tn = _pick(n, tn)
    in_specs = [pl.BlockSpec((tm, k), lambda i, j: (i, 0)), pl.BlockSpec((k, tn), lambda i, j: (0, j))]
    args = [x, w]
    kern = _mm_kernel
    if merge is not None:
        kern = _mm_merge_kernel
        t_att, gates = merge
        nb = n // tn
        in_specs += [
            pl.BlockSpec((tm, tn), lambda i, j: (i, j)),
            pl.BlockSpec((tm, tn), lambda i, j: (i, j)),
            pl.BlockSpec((tm, tn), lambda i, j: (i, j + nb)),
        ]
        args += [t_att, gates, gates]
    return pl.pallas_call(
        kern,
        grid=(m // tm, n // tn),
        in_specs=in_specs,
        out_specs=pl.BlockSpec((tm, tn), lambda i, j: (i, j)),
        out_shape=jax.ShapeDtypeStruct((m, n), out_dtype),
        compiler_params=_cparams(("parallel", "parallel")),
        name=name,
    )(*args)


def _t5_bucket(rel):
    half = N_BUCKETS // 2
    exact = half // 2
    n = jnp.abs(rel)
    far = exact + (
        jnp.log(jnp.maximum(n, 1).astype(F32) / exact) / math.log(MAX_DISTANCE / exact) * (half - exact)
    ).astype(jnp.int32)
    far = jnp.minimum(far, half - 1)
    return jnp.where(rel > 0, half, 0) + jnp.where(n < exact, n, far)


def _attn_kernel(lam_ref, far_ref, q_ref, k_ref, vt_ref, bias_ref, g_ref, o_ref, q2_scr, sa_scr, sb_scr, m_scr, acc_scr, *, t,
                 nk, out_scale):
    h = pl.program_id(1)
    qi = pl.program_id(2)
    qt = (q_ref[...].astype(F32) * (ATT_QK_DIM ** -0.5)).T
    row = lax.broadcasted_iota(jnp.int32, qt.shape, 0)
    q2_scr[:, 0:t] = jnp.where(row < ATT_QK_DIM, qt, 0.0).astype(BF16)
    q2_scr[:, t:2 * t] = jnp.where(row >= ATT_QK_DIM, qt, 0.0).astype(BF16)
    m_scr[...] = jnp.full(m_scr.shape, -jnp.inf, F32)
    acc_scr[...] = jnp.zeros(acc_scr.shape, F32)

    gw = ATTN_GROUP
    ngroups = 2 * t // gw

    def scores(kc, g):
        start = pl.multiple_of(kc * t, t)
        return jnp.dot(k_ref[pl.ds(start, t), :], q2_scr[:, g * gw:(g + 1) * gw],
                       preferred_element_type=F32)

    def update(g, st, vt, const, tile_idx):
        cols = slice(g * gw, (g + 1) * gw)
        if tile_idx is not None:
            b0 = (g * gw) % t
            st = st + bias_ref[tile_idx, :, b0:b0 + gw]
        m_cur = jnp.max(st, axis=0, keepdims=True)
        if const is not None:
            m_cur = m_cur + const
        m_old = m_scr[:, cols]
        m_new = jnp.maximum(m_old, m_cur)
        alpha = jnp.exp(m_old - m_new)
        shift = -m_new if const is None else const - m_new
        pt = jnp.exp(st + shift)
        acc_scr[:, cols] = alpha * acc_scr[:, cols] + jnp.dot(vt, pt.astype(BF16), preferred_element_type=F32)
        m_scr[:, cols] = m_new

    lo_end = jnp.maximum(qi - 1, 0)
    near_end = jnp.minimum(qi + 2, nk)
    three = (near_end - lo_end) == 3
    t_start = lo_end - jnp.where(three & (qi + 2 > nk - 1), 1, 0)
    n_tile = jnp.where(three, 4, 2)
    n_const = nk - n_tile

    def tile_chunk(j):
        kc = jnp.minimum(t_start + j, nk - 1)
        rel = kc - qi
        return kc, jnp.where(rel < -1, 3, jnp.where(rel > 1, 4, rel + 1))

    def const_chunk(f):
        f = jnp.minimum(f, n_const - 1)
        lo = f < t_start
        return jnp.where(lo, f, f + n_tile), jnp.where(lo, far_ref[h, 0], far_ref[h, 1])

    def pipeline(n_chunks, chunk_of, use_tile):
        @pl.when(n_chunks >= 2)
        def _():
            k0, _ = chunk_of(0)
            for g in range(ngroups):
                sa_scr[:, g * gw:(g + 1) * gw] = scores(k0, g)

        def pair_body(j, carry):
            (ka, xa), (kb, xb), (kn, _) = chunk_of(2 * j), chunk_of(2 * j + 1), chunk_of(2 * j + 2)
            for src, dst, k_cur, k_nxt, x in ((sa_scr, sb_scr, ka, kb, xa), (sb_scr, sa_scr, kb, kn, xb)):
                vt = vt_ref[k_cur]
                for g in range(ngroups):
                    cols = slice(g * gw, (g + 1) * gw)
                    dst[:, cols] = scores(k_nxt, g)
                    if use_tile:
                        update(g, src[:, cols], vt, None, x)
                    else:
                        update(g, src[:, cols], vt, x, None)
            return carry

        lax.fori_loop(0, n_chunks // 2, pair_body, 0)

    pipeline(n_tile, tile_chunk, True)
    pipeline(n_const, const_chunk, False)

    lam = lam_ref[0]
    dv = ATT_V_DIM
    ot = (acc_scr[0:dv, 0:t] / acc_scr[dv:dv + 1, 0:t]
          - lam * (acc_scr[0:dv, t:2 * t] / acc_scr[dv:dv + 1, t:2 * t]))
    o = ot.T
    y = o * lax.rsqrt(jnp.mean(o * o, axis=-1, keepdims=True) + EPS)
    o_ref[...] = ((y * g_ref[...]) * out_scale).astype(o_ref.dtype)


def _diff_attention(qkv, b, s, rel_bias, lam, subln, out_scale, t):
    nk = s // t
    hq = ATT_HEADS
    r = jnp.arange(t, dtype=jnp.int32)
    rel = (jnp.arange(3, dtype=jnp.int32)[:, None, None] - 1) * t + r[None, :, None] - r[None, None, :]
    bucket = _t5_bucket(rel)[None]
    rb32 = rel_bias.astype(F32)
    bias_tiles = jnp.zeros((hq, 3, t, t), F32)
    for n in range(N_BUCKETS):
        bias_tiles = jnp.where(bucket == n, rb32[n][:, None, None, None], bias_tiles)
    half = N_BUCKETS // 2
    far = jnp.stack([rel_bias[half - 1], rel_bias[N_BUCKETS - 1]], axis=-1).astype(F32)
    bias_tiles = jnp.concatenate([bias_tiles, jnp.broadcast_to(far[:, :, None, None], (hq, 2, t, t))], axis=1)
    assert nk % 2 == 0 and t % ATTN_GROUP == 0
    vt = qkv[:, 2 * Q_COLS:].reshape(b, nk, t, hq, ATT_V_DIM).transpose(0, 3, 1, 4, 2)
    vt = jnp.concatenate([vt, jnp.ones((b, hq, nk, ONES_ROWS, t), BF16)], axis=3)
    kern = functools.partial(_attn_kernel, t=t, nk=nk, out_scale=out_scale)
    return pl.pallas_call(
        kern,
        grid=(b, hq, nk),
        in_specs=[
            pl.BlockSpec(memory_space=pltpu.SMEM),
            pl.BlockSpec(memory_space=pltpu.SMEM),
            pl.BlockSpec((t, LANES), lambda bi, h, qi: (bi * nk + qi, h)),
            pl.BlockSpec((s, LANES), lambda bi, h, qi: (bi, hq + h)),
            pl.BlockSpec((None, None, nk, ATT_V_DIM + ONES_ROWS, t), lambda bi, h, qi: (bi, h, 0, 0, 0)),
            pl.BlockSpec((None, 5, t, t), lambda bi, h, qi: (h, 0, 0, 0)),
            pl.BlockSpec((1, LANES), lambda bi, h, qi: (0, 0)),
        ],
        out_specs=pl.BlockSpec((t, LANES), lambda bi, h, qi: (bi * nk + qi, h)),
        out_shape=jax.ShapeDtypeStruct((b * s, V_COLS), BF16),
        scratch_shapes=[
            pltpu.VMEM((LANES, 2 * t), BF16),
            pltpu.VMEM((t, 2 * t), F32),
            pltpu.VMEM((t, 2 * t), F32),
            pltpu.VMEM((1, 2 * t), F32),
            pltpu.VMEM((ATT_V_DIM + ONES_ROWS, 2 * t), F32),
        ],
        compiler_params=_cparams(("parallel", "parallel", "parallel")),
        name="diff_attention",
    )(lam.reshape(1).astype(F32), far, qkv, qkv, vt, bias_tiles, subln.reshape(1, LANES).astype(F32))


def _conv_kernel(prev_ref, cur_ref, next_ref, w_ref, b_ref, o_ref, ext_scr, *, tb, ns):
    si = pl.program_id(1)
    ext_scr[0:HALO, :] = jnp.where(si > 0, prev_ref[...], 0.0)
    ext_scr[HALO:HALO + tb, :] = cur_ref[...]
    ext_scr[HALO + tb:2 * HALO + tb, :] = jnp.where(si < ns - 1, next_ref[...], 0.0)
    y = b_ref[...]
    for j in range(D_CONV):
        off = HALO - CONV_PAD + j
        y = y + w_ref[j:j + 1, :] * ext_scr[off:off + tb, :]
    o_ref[...] = y * jax.nn.sigmoid(y)


def _conv_silu(xbc, b, s, conv_w, conv_b, tb=1024, tc=CONV_DIM // 2):
    tb = _pick(s, tb)
    ns = s // tb
    hb = tb // HALO
    nc = CONV_DIM // tc
    kern = functools.partial(_conv_kernel, tb=tb, ns=ns)
    return pl.pallas_call(
        kern,
        grid=(b, ns, nc),
        in_specs=[
            pl.BlockSpec((HALO, tc), lambda bi, si, ci: (jnp.maximum((bi * ns + si) * hb - 1, 0), ci)),
            pl.BlockSpec((tb, tc), lambda bi, si, ci: (bi * ns + si, ci)),
            pl.BlockSpec((HALO, tc), lambda bi, si, ci: (jnp.minimum((bi * ns + si + 1) * hb, b * ns * hb - 1), ci)),
            pl.BlockSpec((D_CONV, tc), lambda bi, si, ci: (0, ci)),
            pl.BlockSpec((1, tc), lambda bi, si, ci: (0, ci)),
        ],
        out_specs=pl.BlockSpec((tb, tc), lambda bi, si, ci: (bi * ns + si, ci)),
        out_shape=jax.ShapeDtypeStruct(xbc.shape, F32),
        scratch_shapes=[pltpu.VMEM((tb + 2 * HALO, tc), F32)],
        compiler_params=_cparams(("parallel", "parallel", "parallel")),
        name="conv_silu",
    )(xbc, xbc, xbc, conv_w.astype(F32), conv_b.reshape(1, CONV_DIM).astype(F32))


def _softplus(x):
    return jnp.maximum(x, 0.0) + jnp.log1p(jnp.exp(-jnp.abs(x)))


def _ssd_kernel(*refs, reverse):
    if reverse:
        x_ref, dt_ref, dtb_ref, a_ref, yf_ref, z_ref, dsk_ref, gn_ref, o_ref, st_scr = refs
    else:
        x_ref, dt_ref, dtb_ref, a_ref, o_ref, st_scr = refs
    L = SSM_CHUNK
    d = 1 if reverse else 0

    @pl.when(pl.program_id(1) == 0)
    def _():
        st_scr[...] = jnp.zeros(st_scr.shape, F32)

    dt = _softplus(dt_ref[...] + dtb_ref[...])
    a = dt * a_ref[...]
    row = lax.broadcasted_iota(jnp.int32, (L, L), 0)
    col = lax.broadcasted_iota(jnp.int32, (L, L), 1)
    tril = (row >= col).astype(F32)
    acs = jnp.dot(tril, a, preferred_element_type=F32, precision=lax.Precision.HIGHEST)
    ecs = acs - a if reverse else acs
    ecs_t = ecs.T
    keep = (col >= row) if reverse else (row >= col)
    total = jnp.broadcast_to(acs[L - 1:L, :], (SUBLANES, LANES))
    if reverse:
        out_decay = jnp.exp(total[0:1, :] - ecs)
        in_decay = jnp.exp(ecs)
    else:
        out_decay = jnp.exp(ecs)
        in_decay = jnp.exp(total[0:1, :] - ecs)

    ch_head = lax.broadcasted_iota(jnp.int32, (LANES, D_INNER), 1) // SSM_HEADDIM + d * SSM_HEADS
    spread = jnp.where(lax.broadcasted_iota(jnp.int32, (LANES, D_INNER), 0) == ch_head, 1.0, 0.0).astype(BF16)

    def per_channel(v):
        hi = v.astype(BF16)
        lo = (v - hi.astype(F32)).astype(BF16)
        return (jnp.dot(hi, spread, preferred_element_type=F32) + jnp.dot(lo, spread, preferred_element_type=F32))

    xdt = x_ref[:, 0:D_INNER] * per_channel(dt)
    xdt16 = xdt.astype(BF16)
    xin16 = (xdt * per_channel(in_decay)).astype(BF16)
    out_ch = per_channel(out_decay)
    tot_ch = per_channel(jnp.exp(total))[0:1, :]
    first = lax.broadcasted_iota(jnp.int32, (L, LANES), 1) < SSM_HEADDIM

    ys = []
    for g in range(SSM_GROUPS):
        bg = x_ref[:, D_INNER + g * D_STATE:D_INNER + (g + 1) * D_STATE]
        cg = x_ref[:, D_INNER + SSM_GROUPS * D_STATE + g * D_STATE:D_INNER + SSM_GROUPS * D_STATE + (g + 1) * D_STATE]
        cg16 = cg.astype(BF16)
        cb = lax.dot_general(cg16, bg.astype(BF16), (((1,), (1,)), ((), ())), preferred_element_type=F32)
        bgt16 = bg.T.astype(BF16)
        for pr in range(HEADS_PER_GROUP // 2):
            pair = g * (HEADS_PER_GROUP // 2) + pr
            lanes = slice(pair * LANES, (pair + 1) * LANES)
            xp = xdt16[:, lanes]
            halves = (jnp.where(first, xp, jnp.zeros_like(xp)), jnp.where(first, jnp.zeros_like(xp), xp))
            y = None
            for k in range(2):
                c = d * SSM_HEADS + 2 * pair + k
                e_col = ecs[:, c:c + 1]
                e_row = ecs_t[c:c + 1, :]
                diff = e_row - e_col if reverse else e_col - e_row
                seg = jnp.exp(jnp.where(keep, diff, -jnp.inf))
                yk = jnp.dot((seg * cb).astype(BF16), halves[k], preferred_element_type=F32)
                y = yk if y is None else y + yk
            st = st_scr[pair]
            y = y + jnp.dot(cg16, st.astype(BF16), preferred_element_type=F32) * out_ch[:, lanes]
            st_scr[pair] = st * tot_ch[:, lanes] + jnp.dot(bgt16, xin16[:, lanes], preferred_element_type=F32)
            ys.append(y)
    y = jnp.concatenate(ys, axis=-1)
    if not reverse:
        o_ref[...] = y
        return
    xs = x_ref[:, 0:D_INNER]
    y = y + yf_ref[...] + dsk_ref[...] * xs
    z = z_ref[...]
    y = y * (z * jax.nn.sigmoid(z))
    gw = D_INNER // SSM_GROUPS
    outs = []
    for g in range(SSM_GROUPS):
        yg = y[:, g * gw:(g + 1) * gw]
        yn = yg * lax.rsqrt(jnp.mean(yg * yg, axis=-1, keepdims=True) + EPS)
        outs.append(yn * gn_ref[:, g * gw:(g + 1) * gw])
    o_ref[...] = jnp.concatenate(outs, axis=-1).astype(o_ref.dtype)


def _ssd(xbc_act, dt_raw, z, b, s, dt_bias, a_log, d_skip, ssm_norm):
    L = SSM_CHUNK
    nc = s // L
    pad = LANES - DT_COLS
    dtb = jnp.pad(dt_bias.astype(F32).reshape(1, DT_COLS), ((0, 0), (0, pad)))
    a_neg = jnp.pad(-jnp.exp(a_log.astype(F32)).reshape(1, DT_COLS), ((0, 0), (0, pad)))
    dsk = d_skip.astype(F32)
    dsk = jnp.repeat(dsk[0] + dsk[1], SSM_HEADDIM).reshape(1, D_INNER)
    st_shape = pltpu.VMEM((SSM_HEADS // 2, D_STATE, 2 * SSM_HEADDIM), F32)
    small = lambda w: pl.BlockSpec((1, w), lambda bi, ci: (0, 0))

    fwd_map = lambda bi, ci: (bi * nc + ci, 0)
    y_f = pl.pallas_call(
        functools.partial(_ssd_kernel, reverse=False),
        grid=(b, nc),
        in_specs=[pl.BlockSpec((L, CONV_DIM), fwd_map), pl.BlockSpec((L, LANES), fwd_map), small(LANES), small(LANES)],
        out_specs=pl.BlockSpec((L, D_INNER), fwd_map),
        out_shape=jax.ShapeDtypeStruct((b * s, D_INNER), F32),
        scratch_shapes=[st_shape],
        compiler_params=_cparams(("parallel", "arbitrary")),
        name="ssd_fwd",
    )(xbc_act, dt_raw, dtb, a_neg)

    bwd_map = lambda bi, ci: (bi * nc + (nc - 1 - ci), 0)
    return pl.pallas_call(
        functools.partial(_ssd_kernel, reverse=True),
        grid=(b, nc),
        in_specs=[
            pl.BlockSpec((L, CONV_DIM), bwd_map),
            pl.BlockSpec((L, LANES), bwd_map),
            small(LANES),
            small(LANES),
            pl.BlockSpec((L, D_INNER), bwd_map),
            pl.BlockSpec((L, D_INNER), bwd_map),
            small(D_INNER),
            small(D_INNER),
        ],
        out_specs=pl.BlockSpec((L, D_INNER), bwd_map),
        out_shape=jax.ShapeDtypeStruct((b * s, D_INNER), BF16),
        scratch_shapes=[st_shape],
        compiler_params=_cparams(("parallel", "arbitrary")),
        name="ssd_bwd",
    )(xbc_act, dt_raw, dtb, a_neg, y_f, z, dsk, ssm_norm.astype(F32).reshape(1, D_INNER))


def _cross_kernel(q_ref, k_ref, v_ref, o_ref):
    outs = []
    for h in range(CROSS_HEADS):
        sl = slice(h * CROSS_DIM, (h + 1) * CROSS_DIM)
        s = lax.dot_general(q_ref[:, sl], k_ref[:, sl], (((1,), (1,)), ((), ())), preferred_element_type=F32)
        s = s * (CROSS_DIM ** -0.5)
        p = jnp.exp(s - jnp.max(s, axis=-1, keepdims=True))
        p = p / jnp.sum(p, axis=-1, keepdims=True)
        outs.append(jnp.dot(p.astype(BF16), v_ref[:, sl], preferred_element_type=F32))
    o_ref[...] = jnp.concatenate(outs, axis=-1).astype(o_ref.dtype)


def _cross_core(q, kv, b, s, n_mem, tq=512):
    tq = _pick(s, tq)
    nq = s // tq
    return pl.pallas_call(
        _cross_kernel,
        grid=(b, nq),
        in_specs=[
            pl.BlockSpec((tq, D_MODEL), lambda bi, qi: (bi * nq + qi, 0)),
            pl.BlockSpec((n_mem, D_MODEL), lambda bi, qi: (bi, 0)),
            pl.BlockSpec((n_mem, D_MODEL), lambda bi, qi: (bi, 1)),
        ],
        out_specs=pl.BlockSpec((tq, D_MODEL), lambda bi, qi: (bi * nq + qi, 0)),
        out_shape=jax.ShapeDtypeStruct((b * s, D_MODEL), BF16),
        compiler_params=_cparams(("parallel", "parallel")),
        name="cross_attention",
    )(q, kv, kv)


def _top16(s, vals_scr, idx_scr, payload=None):
    n = s.shape[0]
    pos = lax.broadcasted_iota(jnp.int32, s.shape, 0).astype(F32)
    for k in range(PEER_TOPK):
        m = jnp.max(s, axis=0, keepdims=True)
        first = jnp.min(jnp.where(s == m, pos, float(n)), axis=0, keepdims=True)
        sel = pos == first
        vals_scr[k:k + 1, :] = m
        if payload is None:
            idx_scr[k:k + 1, :] = first
        else:
            idx_scr[k:k + 1, :] = jnp.max(jnp.where(sel, payload, -1.0), axis=0, keepdims=True)
        s = jnp.where(sel, -jnp.inf, s)


def _topk_kernel(q_ref, keys_ref, idx_ref, gate_ref, v1_scr, i1_scr, v2_scr, i2_scr, tv_scr, ti_scr):
    for h in range(PEER_HEADS):
        qh = q_ref[:, h * D_KEY:(h + 1) * D_KEY]
        sc = lax.dot_general(keys_ref[h], qh, (((1,), (1,)), ((), ())), preferred_element_type=F32)
        _top16(sc[0:N_KEYS], v1_scr, i1_scr)
        _top16(sc[N_KEYS:2 * N_KEYS], v2_scr, i2_scr)
        hs = SUBLANES
        pairs = [(slice(0, 1), slice(0, hs)), (slice(0, 1), slice(hs, 2 * hs))]
        pairs += [(slice(i, i + 1), slice(0, hs)) for i in range(1, hs)]
        pairs += [(slice(hs, 2 * hs), slice(0, 1))]
        cand = jnp.concatenate([v1_scr[a, :] + v2_scr[b, :] for a, b in pairs], axis=0)
        eidx = jnp.concatenate([i1_scr[a, :] * N_KEYS + i2_scr[b, :] for a, b in pairs], axis=0)
        _top16(cand, tv_scr, ti_scr, payload=eidx)
        top = tv_scr[...]
        e = jnp.exp(top - top[0:1, :])
        gate_ref[h * PEER_TOPK:(h + 1) * PEER_TOPK, :] = e / jnp.sum(e, axis=0, keepdims=True)
        idx_ref[h * PEER_TOPK:(h + 1) * PEER_TOPK, :] = ti_scr[...]


def _peer_topk(pq, keys_bd, tb=128):
    tt = pq.shape[0]
    nhk = PEER_HEADS * PEER_TOPK
    sc16 = lambda: pltpu.VMEM((PEER_TOPK, tb), F32)
    return pl.pallas_call(
        _topk_kernel,
        grid=(tt // tb,),
        in_specs=[
            pl.BlockSpec((tb, PEER_HEADS * D_KEY), lambda i: (i, 0)),
            pl.BlockSpec((PEER_HEADS, 2 * N_KEYS, D_KEY), lambda i: (0, 0, 0)),
        ],
        out_specs=[pl.BlockSpec((nhk, tb), lambda i: (0, i)), pl.BlockSpec((nhk, tb), lambda i: (0, i))],
        out_shape=[jax.ShapeDtypeStruct((nhk, tt), F32), jax.ShapeDtypeStruct((nhk, tt), F32)],
        scratch_shapes=[sc16() for _ in range(6)],
        compiler_params=_cparams(("parallel",)),
        name="peer_topk",
    )(pq, keys_bd)


GATE_GROUP = 16
GATE_PITCH = N_KEYS + 8


def _gate_kernel(idx_ref, gate_ref, o_ref, idx_scr, gate_scr, ga_scr, gb_scr, *, tg):
    idx_scr[...] = idx_ref[...].T
    gate_scr[...] = gate_ref[...].T
    pos = lax.broadcasted_iota(jnp.int32, (N_KEYS, PEER_HEADS * PEER_TOPK), 0)
    half = GATE_GROUP // 2
    ngroups = tg // GATE_GROUP

    def tiles(gi, dst):
        base = pl.multiple_of(gi * GATE_GROUP, GATE_GROUP)
        for u in range(GATE_GROUP):
            e = idx_scr[pl.ds(base + u, 1), :].astype(jnp.int32)
            g = gate_scr[pl.ds(base + u, 1), :]
            a_t = jnp.where(pos == (e >> 7), 1.0, 0.0).astype(BF16)
            b_t = jnp.where(pos == (e & (N_KEYS - 1)), g, 0.0).astype(BF16)
            dst[u * GATE_PITCH:u * GATE_PITCH + N_KEYS, :] = lax.dot_general(
                a_t, b_t, (((1,), (1,)), ((), ())), preferred_element_type=F32)

    def relayout(gi, src):
        base = pl.multiple_of(gi * GATE_GROUP, GATE_GROUP)
        for e1 in range(N_KEYS):
            lo = src[pl.ds(e1, half, stride=GATE_PITCH), :]
            hi = src[pl.ds(half * GATE_PITCH + e1, half, stride=GATE_PITCH), :]
            o_ref[e1, pl.ds(base, GATE_GROUP), :] = jnp.concatenate([lo, hi], axis=0).astype(o_ref.dtype)

    tiles(0, ga_scr)

    def body(j, carry):
        tiles(2 * j + 1, gb_scr)
        relayout(2 * j, ga_scr)
        tiles(jnp.minimum(2 * j + 2, ngroups - 1), ga_scr)
        relayout(2 * j + 1, gb_scr)
        return carry

    lax.fori_loop(0, ngroups // 2, body, 0)


def _peer_gates(idx, gate, tg=256):
    nhk, tt = idx.shape
    return pl.pallas_call(
        functools.partial(_gate_kernel, tg=tg),
        grid=(tt // tg,),
        in_specs=[pl.BlockSpec((nhk, tg), lambda i: (0, i)), pl.BlockSpec((nhk, tg), lambda i: (0, i))],
        out_specs=pl.BlockSpec((N_KEYS, tg, N_KEYS), lambda i: (0, i, 0)),
        out_shape=jax.ShapeDtypeStruct((N_KEYS, tt, N_KEYS), BF16),
        scratch_shapes=[
            pltpu.VMEM((tg, nhk), F32),
            pltpu.VMEM((tg, nhk), F32),
            pltpu.VMEM((GATE_GROUP * GATE_PITCH, N_KEYS), F32),
            pltpu.VMEM((GATE_GROUP * GATE_PITCH, N_KEYS), F32),
        ],
        compiler_params=_cparams(("parallel",)),
        name="peer_gates",
    )(idx, gate)


def _gelu_tanh(x):
    return x * (0.5 * (1.0 + jnp.tanh(math.sqrt(2.0 / math.pi) * (x + 0.044715 * (x * x * x)))))


def _ffn_kernel(h_ref, u_ref, v_ref, g_ref, x_ref, nf_ref, o_ref):
    j = pl.program_id(1)

    @pl.when(j == 0)
    def _():
        o_ref[...] = jnp.zeros(o_ref.shape, F32)

    s = jnp.dot(h_ref[...], u_ref[...], preferred_element_type=F32)
    coeff = jnp.concatenate(
        [g_ref[k].astype(F32) * _gelu_tanh(s[:, k * LANES:(k + 1) * LANES]) for k in range(g_ref.shape[0])], axis=-1
    ).astype(BF16)
    o_ref[...] += jnp.dot(coeff, v_ref[...], preferred_element_type=F32)

    @pl.when(j == pl.num_programs(1) - 1)
    def _():
        x = x_ref[...] + o_ref[...]
        y = x * lax.rsqrt(jnp.mean(x * x, axis=-1, keepdims=True) + EPS)
        o_ref[...] = y * nf_ref[...]


def _peer_ffn(hp, ut16, v16, gmat, x_res, norm_final, tm=512, te=1024):
    tt = hp.shape[0]
    tm = _pick(tt, tm)
    return pl.pallas_call(
        _ffn_kernel,
        grid=(tt // tm, N_EXPERTS // te),
        in_specs=[
            pl.BlockSpec((tm, D_MODEL), lambda i, j: (i, 0)),
            pl.BlockSpec((D_MODEL, te), lambda i, j: (0, j)),
            pl.BlockSpec((te, D_MODEL), lambda i, j: (j, 0)),
            pl.BlockSpec((te // LANES, tm, LANES), lambda i, j: (j, i, 0)),
            pl.BlockSpec((tm, D_MODEL), lambda i, j: (i, 0)),
            pl.BlockSpec((1, D_MODEL), lambda i, j: (0, 0)),
        ],
        out_specs=pl.BlockSpec((tm, D_MODEL), lambda i, j: (i, 0)),
        out_shape=jax.ShapeDtypeStruct((tt, D_MODEL), F32),
        compiler_params=_cparams(("parallel", "arbitrary")),
        name="peer_ffn",
    )(hp, ut16, v16, gmat, x_res, norm_final.reshape(1, D_MODEL).astype(F32))


def _prepare(p):
    w_in = p["w_in"][0]
    c = np.cumsum([0, 3 * Q_COLS, D_INNER, CONV_DIM, DT_COLS, 2 * D_MODEL])
    w = {}
    w["qkv"] = w_in[:, c[0]:c[1]].astype(BF16)
    w["z"] = w_in[:, c[1]:c[2]].astype(BF16)
    w["xbc"] = w_in[:, c[2]:c[3]].astype(BF16)
    w["dt"] = jnp.pad(w_in[:, c[3]:c[4]], ((0, 0), (0, LANES - DT_COLS))).astype(BF16)
    w["gates"] = w_in[:, c[4]:c[5]].astype(BF16)
    for name in ("w_attn_o", "w_ssm_o", "w_out", "w_cq", "w_ckv", "w_co", "w_pq", "expert_v"):
        w[name] = p[name][0].astype(BF16)
    w["expert_ut"] = p["expert_u"][0].astype(BF16).T
    sk = p["sub_keys"][0].astype(BF16)
    zero = jnp.zeros_like(sk[:, 0])
    w["keys_bd"] = jnp.concatenate(
        [jnp.concatenate([sk[:, 0], zero], axis=-1), jnp.concatenate([zero, sk[:, 1]], axis=-1)], axis=1
    )
    lam_init = 0.8 - 0.6 * math.exp(-0.3 * 0)
    w["lam"] = (
        jnp.exp(jnp.sum(p["lam_q1"][0].astype(F32) * p["lam_k1"][0].astype(F32)))
        - jnp.exp(jnp.sum(p["lam_q2"][0].astype(F32) * p["lam_k2"][0].astype(F32)))
        + lam_init
    )
    w["lam_init"] = lam_init
    return w


def _trunk(x, mem, p, w):
    b, s, d = x.shape
    n_mem = mem.shape[1]
    xf = x.reshape(b * s, d)

    h1 = _rmsnorm_bf16(xf, p["norm_mix"][0])
    qkv = _mm(h1, w["qkv"], BF16, name="mm_qkv")
    z = _mm(h1, w["z"], F32, name="mm_z")
    xbc = _mm(h1, w["xbc"], F32, tn=512, name="mm_xbc")
    dt_raw = _mm(h1, w["dt"], F32, name="mm_dt")
    gates = _mm(h1, w["gates"], F32, name="mm_gates")

    o_att = _diff_attention(qkv, b, s, p["rel_bias"], w["lam"], p["attn_subln"][0], 1.0 - w["lam_init"],
                            _pick(s, ATTN_TILE))
    xbc_act = _conv_silu(xbc, b, s, p["conv_w"][0], p["conv_b"][0])
    y_ssm = _ssd(xbc_act, dt_raw, z, b, s, p["dt_bias"][0], p["a_log"][0], p["d_skip"][0], p["ssm_norm"][0])

    t_att = _mm(o_att, w["w_attn_o"], F32, name="mm_attn_o")
    merged = _mm(y_ssm, w["w_ssm_o"], BF16, merge=(t_att, gates), name="mm_ssm_o_merge")
    x1, hq = _mm_residual_norm(merged, w["w_out"], xf, p["norm_cross"][0], name="mm_out")

    q = _mm(hq, w["w_cq"], BF16, name="mm_cq")
    mn = _rmsnorm_bf16(mem.reshape(b * n_mem, d), p["norm_mem"][0])
    kv = _mm(mn, w["w_ckv"], BF16, name="mm_ckv")
    oc = _cross_core(q, kv, b, s, n_mem)
    x2, hp = _mm_residual_norm(oc, w["w_co"], x1, p["norm_ffn"][0], name="mm_co")

    pq = _mm(hp, w["w_pq"], BF16, name="mm_pq")
    idx, gate = _peer_topk(pq, w["keys_bd"])
    gmat = _peer_gates(idx, gate)
    y = _peer_ffn(hp, w["expert_ut"], w["expert_v"], gmat, x2, p["norm_final"])
    return y.reshape(b, s, d)


def kernel(x_prompt, x_sample, mem_prompt, mem_sample, norm_mix, w_in, lam_q1, lam_k1, lam_q2, lam_k2, rel_bias, attn_subln, w_attn_o, conv_w, conv_b, a_log, dt_bias, d_skip, ssm_norm, w_ssm_o, w_out, norm_cross, norm_mem, w_cq, w_ckv, w_co, norm_ffn, w_pq, sub_keys, expert_u, expert_v, norm_final):
    p = dict(norm_mix=norm_mix, w_in=w_in, lam_q1=lam_q1, lam_k1=lam_k1, lam_q2=lam_q2, lam_k2=lam_k2,
             rel_bias=rel_bias, attn_subln=attn_subln, w_attn_o=w_attn_o, conv_w=conv_w, conv_b=conv_b,
             a_log=a_log, dt_bias=dt_bias, d_skip=d_skip, ssm_norm=ssm_norm, w_ssm_o=w_ssm_o, w_out=w_out,
             norm_cross=norm_cross, norm_mem=norm_mem, w_cq=w_cq, w_ckv=w_ckv, w_co=w_co, norm_ffn=norm_ffn,
             w_pq=w_pq, sub_keys=sub_keys, expert_u=expert_u, expert_v=expert_v, norm_final=norm_final)
    w = _prepare(p)
    y_prompt = _trunk(x_prompt, mem_prompt, p, w)
    y_sample = _trunk(x_sample, mem_sample, p, w)
    return (y_prompt, y_sample)
```
